```python
import jax, jax.numpy as jnp
from jax import lax
import numpy as np

D_MODEL = 1024
BATCH = 4
SEQ = 8192
DEPTH = 2

GRID_W = 64
CTX_LEN = 256
MIX_WIDTH = D_MODEL
N_MIXERS = 4
GROUP_WIDTH = MIX_WIDTH // N_MIXERS
HEAD_DIM = 64
RWKV_HEADS = GROUP_WIDTH // HEAD_DIM
DECAY_LORA = 64
ICLR_LORA = 64
GATE_LORA = 128
READ_COLS = GATE_LORA + GROUP_WIDTH
STATE_COLS = 2 * GROUP_WIDTH + 2 * DECAY_LORA + 2 * ICLR_LORA
RWKV_COLS = READ_COLS + STATE_COLS
POOL_WINDOWS = (2, 4, 8, 16)
POOL_GROUP_DIM = GROUP_WIDTH // len(POOL_WINDOWS)
FOURIER_GROUPS = 4
FOURIER_GROUP_DIM = GROUP_WIDTH // FOURIER_GROUPS
CONV_WIDTH = 31
N_IN = RWKV_COLS + GROUP_WIDTH + GROUP_WIDTH + 2 * GROUP_WIDTH
D_FF = -(-8 * D_MODEL // (3 * 256)) * 256
RMS_EPS = 1e-6
GN_EPS = 64e-5
LN_EPS = 1e-5
POS_BASE = 10000.0

kernel_name = 'hybrid_rwkv7_pool_fnet_conformer_dit'


def _rmsnorm(x, g):
    xf = x.astype(jnp.float32)
    y = xf * lax.rsqrt(jnp.mean(xf * xf, axis=-1, keepdims=True) + RMS_EPS)
    return (y * g.astype(jnp.float32)).astype(x.dtype)


def _modulate(x, g, shift, scale):
    return _rmsnorm(x, g) * (1 + scale) + shift


def _pos_embed_2d(rows, dim):
    quarter = dim // 4
    omega = 1.0 / (POS_BASE ** (jnp.arange(quarter, dtype=jnp.float32) / quarter))
    row = jnp.repeat(jnp.arange(rows, dtype=jnp.float32), GRID_W)
    col = jnp.tile(jnp.arange(GRID_W, dtype=jnp.float32), rows)

    def enc(p):
        ang = p[:, None] * omega[None, :]
        return jnp.concatenate([jnp.sin(ang), jnp.cos(ang)], axis=-1)

    return jnp.concatenate([enc(row), enc(col)], axis=-1)


def _token_shift(u, mu_prev, mu_next):
    pad = jnp.zeros_like(u[:, :1])
    prev = jnp.concatenate([pad, u[:, :-1]], axis=1)
    nxt = jnp.concatenate([u[:, 1:], pad], axis=1)
    return u + mu_prev * (prev - u) + mu_next * (nxt - u)


def _heads(z):
    return z.reshape(z.shape[:-1] + (RWKV_HEADS, HEAD_DIM))


def _rwkv_inputs(s, w0, w2, a0, a2, k_k, k_a):
    G = GROUP_WIDTH
    B, T, _ = s.shape
    k, v = s[..., :G], s[..., G:2 * G]
    wl = s[..., 2 * G:2 * G + 2 * DECAY_LORA].reshape(B, T, 2, DECAY_LORA)
    al = s[..., 2 * G + 2 * DECAY_LORA:].reshape(B, T, 2, ICLR_LORA)
    w = -jax.nn.softplus(-(w0 + jnp.einsum('btdr,drc->btdc', jnp.tanh(wl), w2))) - 0.5
    decay = jnp.exp(-jnp.exp(w))
    a = jax.nn.sigmoid(a0 + jnp.einsum('btdr,drc->btdc', al, a2))
    kk = _heads(k * k_k)
    kk = kk / jnp.maximum(jnp.sqrt(jnp.sum(kk * kk, axis=-1, keepdims=True)), 1e-12)
    k_dir = k[:, :, None, :] * (1.0 + (a - 1.0) * k_a)
    return _heads(decay), kk, _heads(a), _heads(k_dir), _heads(v)


def _wkv_scan(S0, decay, kk, a, k, v, r, reverse):
    with_out = r is not None

    def step(S, inp):
        w_t, kk_t, a_t, k_t, v_t = inp[:5]
        sa = jnp.einsum('bhvk,bhk->bhv', S, -kk_t)
        S = (S * w_t[:, :, None, :] + sa[..., None] * (kk_t * a_t)[:, :, None, :]
             + v_t[..., None] * k_t[:, :, None, :])
        y = jnp.einsum('bhvk,bhk->bhv', S, inp[5]) if with_out else None
        return S, y

    seq = (decay, kk, a, k, v) + ((r,) if with_out else ())
    S, ys = lax.scan(step, S0, tuple(jnp.swapaxes(z, 0, 1) for z in seq), reverse=reverse)
    return S, (jnp.swapaxes(ys, 0, 1) if with_out else None)


def _rwkv_readout(y, r, k_dirs, v, gl, g2, r_k, gn_w, gn_b):
    B, T = y.shape[:2]
    mu = jnp.mean(y, axis=-1, keepdims=True)
    var = jnp.mean(jnp.square(y - mu), axis=-1, keepdims=True)
    yn = ((y - mu) * lax.rsqrt(var + GN_EPS)).reshape(B, T, GROUP_WIDTH) * gn_w + gn_b
    coef = jnp.sum(jnp.sum(r[:, :, None] * k_dirs * r_k, axis=-1), axis=2)
    bonus = (coef[..., None] * v).reshape(B, T, GROUP_WIDTH)
    g = jax.nn.sigmoid(gl) @ g2
    return (yn + bonus) * g


def _rwkv_mixer(ux, uc, ctx_out, mu_prev, mu_next, w0, w2, a0, a2, g2, k_k, k_a, r_k, gn_w, gn_b):
    dtype = ux.dtype
    sx = _token_shift(ux.astype(jnp.float32), mu_prev, mu_next)
    wc = uc.shape[-1]
    sc = _token_shift(uc.astype(jnp.float32), mu_prev[-wc:], mu_next[-wc:])
    lw = (w0, w2, a0, a2, k_k, k_a)
    dx, kkx, ax, kx, vx = _rwkv_inputs(sx[..., READ_COLS:], *lw)
    dc, kkc, ac, kc, vc = _rwkv_inputs(sc[..., wc - STATE_COLS:], *lw)
    rx = _heads(sx[..., GATE_LORA:READ_COLS])
    rc = _heads(sc[..., GATE_LORA:READ_COLS]) if ctx_out else None
    S0 = jnp.zeros((ux.shape[0], RWKV_HEADS, HEAD_DIM, HEAD_DIM), jnp.float32)
    y_fwd_x = y_bwd_x = y_fwd_c = y_bwd_c = None
    for d, rev in enumerate((False, True)):
        S_ctx, yc = _wkv_scan(S0, dc[:, :, d], kkc, ac[:, :, d], kc[:, :, d], vc, rc, rev)
        _, yx = _wkv_scan(S_ctx, dx[:, :, d], kkx, ax[:, :, d], kx[:, :, d], vx, rx, rev)
        if d == 0:
            y_fwd_x, y_fwd_c = yx, yc
        else:
            y_bwd_x, y_bwd_c = yx, yc
    ro = (g2, r_k, gn_w, gn_b)
    out_x = _rwkv_readout(y_fwd_x + y_bwd_x, rx, kx, vx, sx[..., :GATE_LORA], *ro).astype(dtype)
    if not ctx_out:
        return out_x, None
    out_c = _rwkv_readout(y_fwd_c + y_bwd_c, rc, kc, vc, sc[..., :GATE_LORA], *ro).astype(dtype)
    return out_x, out_c


def _pool_mixer(u, pool_w, pool_scale):
    B, T, C = u.shape
    uf = u.astype(jnp.float32)
    cz = jnp.concatenate([jnp.zeros_like(uf[:, :1]), jnp.cumsum(uf, axis=1)], axis=1)
    t = jnp.arange(T)
    diffs = []
    for gi, w in enumerate(POOL_WINDOWS):
        sl = slice(gi * POOL_GROUP_DIM, (gi + 1) * POOL_GROUP_DIM)
        cg = cz[..., sl]
        hw = w // 2
        lo0, hi0 = jnp.clip(t - hw, 0, T), jnp.clip(t + hw, 0, T)
        lo1, hi1 = jnp.clip(t - hw + 1, 0, T), jnp.clip(t + hw + 1, 0, T)
        total = (cg[:, hi0] - cg[:, lo0]) + (cg[:, hi1] - cg[:, lo1])
        count = ((hi0 - lo0) + (hi1 - lo1)).astype(jnp.float32)
        diffs.append(total / count[None, :, None] - uf[..., sl])
    dmat = jnp.stack(diffs, axis=2)
    y = jnp.einsum('btgc,gcd->btgd', dmat, pool_w).reshape(B, T, C)
    return (y * pool_scale).astype(u.dtype)


def _fourier_mixer(u, fourier_w):
    B, T, C = u.shape
    z = u.astype(jnp.float32).reshape(B, T, FOURIER_GROUPS, FOURIER_GROUP_DIM)
    f = jnp.fft.fftn(z, axes=(1, 3), norm='ortho').real.reshape(B, T, C)
    return f.astype(u.dtype) @ fourier_w


def _conv_module(u, dw_w, dw_b, ln_g, ln_b, pw):
    a, b = jnp.split(u, 2, axis=-1)
    h = a * jax.nn.sigmoid(b)
    C = h.shape[-1]
    h = lax.conv_general_dilated(h, dw_w[:, None, :], window_strides=(1,), padding='SAME',
                                 dimension_numbers=('NWC', 'WIO', 'NWC'),
                                 feature_group_count=C) + dw_b
    hf = h.astype(jnp.float32)
    mu = jnp.mean(hf, axis=-1, keepdims=True)
    var = jnp.mean(jnp.square(hf - mu), axis=-1, keepdims=True)
    hn = (hf - mu) * lax.rsqrt(var + LN_EPS) * ln_g + ln_b
    return jax.nn.silu(hn).astype(u.dtype) @ pw


def _mix_and_project(u, y_rwkv, pool_w, pool_scale, fourier_w, dw_w, dw_b, ln_g, ln_b, pw, w_out):
    o, G = RWKV_COLS, GROUP_WIDTH
    y_pool = _pool_mixer(u[..., o:o + G], pool_w, pool_scale)
    y_four = _fourier_mixer(u[..., o + G:o + 2 * G], fourier_w)
    y_conv = _conv_module(u[..., o + 2 * G:o + 4 * G], dw_w, dw_b, ln_g, ln_b, pw)
    return jnp.concatenate([y_rwkv, y_pool, y_four, y_conv], axis=-1) @ w_out


def _swiglu(h, w_in, w_out):
    gate, up = jnp.split(h @ w_in, 2, axis=-1)
    return (jax.nn.silu(gate) * up) @ w_out


def setup_inputs(seed: int = 0) -> dict:
    key = jax.random.key(seed)
    ks = iter(jax.random.split(key, 64))
    L, D, G = DEPTH, D_MODEL, GROUP_WIDTH

    def nrm(shape, scale):
        return jax.random.normal(next(ks), shape, jnp.float32) * scale

    def uni(shape, lo, hi):
        return jax.random.uniform(next(ks), shape, jnp.float32, lo, hi)

    return {
        'x': nrm((BATCH, SEQ, D), 1.0),
        'c': nrm((BATCH, D), 1.0),
        'ctx': nrm((BATCH, CTX_LEN, D), 1.0),
        'c_ctx': nrm((D,), 1.0),
        'w_mod': nrm((L, D, 6 * D), 0.5 * D ** -0.5),
        'b_mod': nrm((L, 6 * D), 0.02),
        'norm1_g': 1.0 + nrm((L, D), 0.05),
        'norm2_g': 1.0 + nrm((L, D), 0.05),
        'w_in': nrm((L, D, N_IN), D ** -0.5),
        'w_out': nrm((L, MIX_WIDTH, D), MIX_WIDTH ** -0.5),
        'rwkv_mu_prev': uni((L, RWKV_COLS), 0.0, 0.5),
        'rwkv_mu_next': uni((L, RWKV_COLS), 0.0, 0.5),
        'rwkv_w0': uni((L, 2, G), -6.0, -1.0),
        'rwkv_w2': nrm((L, 2, DECAY_LORA, G), 0.1),
        'rwkv_a0': nrm((L, 2, G), 0.1),
        'rwkv_a2': nrm((L, 2, ICLR_LORA, G), 0.1),
        'rwkv_g2': nrm((L, GATE_LORA, G), GATE_LORA ** -0.5),
        'rwkv_k_k': 0.85 + nrm((L, G), 0.05),
        'rwkv_k_a': 1.0 + nrm((L, G), 0.05),
        'rwkv_r_k': nrm((L, RWKV_HEADS, HEAD_DIM), 0.1),
        'rwkv_gn_w': 1.0 + nrm((L, G), 0.05),
        'rwkv_gn_b': nrm((L, G), 0.02),
        'pool_w': nrm((L, len(POOL_WINDOWS), POOL_GROUP_DIM, POOL_GROUP_DIM), POOL_GROUP_DIM ** -0.5),
        'pool_scale': 1.0 + nrm((L, G), 0.1),
        'fourier_w': nrm((L, G, G), G ** -0.5),
        'conv_dw_w': nrm((L, CONV_WIDTH, G), CONV_WIDTH ** -0.5),
        'conv_dw_b': nrm((L, G), 0.02),
        'conv_ln_g': 1.0 + nrm((L, G), 0.05),
        'conv_ln_b': nrm((L, G), 0.02),
        'conv_pw': nrm((L, G, G), G ** -0.5),
        'ffn_w_in': nrm((L, D, 2 * D_FF), D ** -0.5),
        'ffn_w_out': nrm((L, D_FF, D), D_FF ** -0.5),
        'final_norm_g': 1.0 + nrm((D,), 0.05),
    }


def reference(x, c, ctx, c_ctx, w_mod, b_mod, norm1_g, norm2_g, w_in, w_out,
              rwkv_mu_prev, rwkv_mu_next, rwkv_w0, rwkv_w2, rwkv_a0, rwkv_a2, rwkv_g2,
              rwkv_k_k, rwkv_k_a, rwkv_r_k, rwkv_gn_w, rwkv_gn_b,
              pool_w, pool_scale, fourier_w,
              conv_dw_w, conv_dw_b, conv_ln_g, conv_ln_b, conv_pw,
              ffn_w_in, ffn_w_out, final_norm_g):
    B, T, D = x.shape
    rows = T // GRID_W
    h = x + _pos_embed_2d(rows, D).astype(x.dtype)[None]
    hc = ctx
    for l in range(DEPTH):
        last = l == DEPTH - 1
        mod_x = (jax.nn.silu(c) @ w_mod[l] + b_mod[l])[:, None, :]
        sh1, sc1, gt1, sh2, sc2, gt2 = jnp.split(mod_x, 6, axis=-1)
        n_ctx_mod = 2 if last else 6
        mod_c = jax.nn.silu(c_ctx) @ w_mod[l, :, :n_ctx_mod * D] + b_mod[l, :n_ctx_mod * D]
        mc = jnp.split(mod_c, n_ctx_mod)
        ux = _modulate(h, norm1_g[l], sh1, sc1) @ w_in[l]
        ac = _modulate(hc, norm1_g[l], mc[0], mc[1])
        uc = ac @ (w_in[l, :, READ_COLS:RWKV_COLS] if last else w_in[l])
        y_rwkv_x, y_rwkv_c = _rwkv_mixer(
            ux[..., :RWKV_COLS], uc[..., :RWKV_COLS], not last,
            rwkv_mu_prev[l], rwkv_mu_next[l], rwkv_w0[l], rwkv_w2[l], rwkv_a0[l], rwkv_a2[l],
            rwkv_g2[l], rwkv_k_k[l], rwkv_k_a[l], rwkv_r_k[l], rwkv_gn_w[l], rwkv_gn_b[l])
        other = (pool_w[l], pool_scale[l], fourier_w[l], conv_dw_w[l], conv_dw_b[l],
                 conv_ln_g[l], conv_ln_b[l], conv_pw[l], w_out[l])
        h = h + gt1 * _mix_and_project(ux, y_rwkv_x, *other)
        h = h + gt2 * _swiglu(_modulate(h, norm2_g[l], sh2, sc2), ffn_w_in[l], ffn_w_out[l])
        if not last:
            hc = hc + mc[2] * _mix_and_project(uc, y_rwkv_c, *other)
            hc = hc + mc[5] * _swiglu(_modulate(hc, norm2_g[l], mc[3], mc[4]), ffn_w_in[l], ffn_w_out[l])
    return _rmsnorm(h, final_norm_g)
```

```python
import functools
import math

import jax
import jax.numpy as jnp
import numpy as np
from jax import lax
from jax.experimental import pallas as pl
from jax.experimental.pallas import tpu as pltpu

F32 = jnp.float32
BF16 = jnp.bfloat16

GROUP = 256
HEAD = 64
NHEAD = GROUP // HEAD
GATE_LORA = 128
LORA = 64
RWKV_COLS = GATE_LORA + 3 * GROUP + 4 * LORA
GRID_W = 64
POOL_WINDOWS = (2, 4, 8, 16)
CONV_WIDTH = 31
RMS_EPS = 1e-6
GN_EPS = 64e-5
LN_EPS = 1e-5
POS_BASE = 10000.0

CHUNK = 64
HALO = 16
V7X_VMEM_LIMIT = 56 * 1024 * 1024


def _params(*sem):
    return pltpu.CompilerParams(dimension_semantics=sem, vmem_limit_bytes=V7X_VMEM_LIMIT)


def _dot(a, b):
    return jnp.dot(a.astype(BF16), b.astype(BF16), preferred_element_type=F32)


def _dot_nt(a, b):
    return lax.dot_general(a.astype(BF16), b.astype(BF16), (((1,), (1,)), ((), ())),
                           preferred_element_type=F32)


def _dot_tn(a, b):
    return lax.dot_general(a.astype(BF16), b.astype(BF16), (((0,), (0,)), ((), ())),
                           preferred_element_type=F32)


def _split2(x):
    hi = x.astype(BF16)
    lo = (x - hi.astype(F32)).astype(BF16)
    return hi, lo


def _split3(x):
    hi = x.astype(BF16)
    r = x - hi.astype(F32)
    mid = r.astype(BF16)
    lo = (r - mid.astype(F32)).astype(BF16)
    return hi, mid, lo


def _lane_head(shape):
    return lax.broadcasted_iota(jnp.int32, shape, 1) // HEAD


def _bd(xw):
    xb = xw.astype(BF16)
    lh = _lane_head(xb.shape)
    zero = jnp.zeros_like(xb)
    return jnp.concatenate([jnp.where(lh == h, xb, zero) for h in range(NHEAD)], axis=0)


def _compact(full):
    lh = _lane_head((HEAD, GROUP))
    out = jnp.zeros((HEAD, GROUP), F32)
    for h in range(NHEAD):
        out = out + jnp.where(lh == h, full[h * HEAD:(h + 1) * HEAD, :], 0.0)
    return out


def _head_sum(x, ones_bd):
    hi, lo = _split2(x)
    return (jnp.dot(hi, ones_bd, preferred_element_type=F32)
            + jnp.dot(lo, ones_bd, preferred_element_type=F32))


def _ones_bd():
    r = lax.broadcasted_iota(jnp.int32, (GROUP, GROUP), 0) // HEAD
    c = lax.broadcasted_iota(jnp.int32, (GROUP, GROUP), 1) // HEAD
    return jnp.where(r == c, 1.0, 0.0).astype(BF16)


def _rwkv_prep_kernel(u_ref, up_ref, un_ref, mup_ref, mun_ref, w0_ref, w2_ref, a0_ref, a2_ref,
                      g2_ref, kk_ref, ka_ref, rk_ref,
                      logw_ref, kkn_ref, b_ref, kd_ref, v_ref, r_ref, bonus_ref, gate_ref,
                      buf_ref, *, tm):
    i = pl.program_id(1)
    nt = pl.num_programs(1)
    prev_halo = jnp.where(i == 0, 0.0, up_ref[0])
    next_halo = jnp.where(i == nt - 1, 0.0, un_ref[0])
    buf_ref[0:8, :] = prev_halo
    buf_ref[8:8 + tm, :] = u_ref[0]
    buf_ref[8 + tm:16 + tm, :] = next_halo
    u = u_ref[0]
    prev = buf_ref[7:7 + tm, :]
    nxt = buf_ref[9:9 + tm, :]
    s = u + mup_ref[...] * (prev - u) + mun_ref[...] * (nxt - u)

    G = GROUP
    gl = s[:, 0:GATE_LORA]
    r = s[:, GATE_LORA:GATE_LORA + G]
    k = s[:, GATE_LORA + G:GATE_LORA + 2 * G]
    v = s[:, GATE_LORA + 2 * G:GATE_LORA + 3 * G]
    wl = s[:, GATE_LORA + 3 * G:GATE_LORA + 3 * G + 2 * LORA]
    al = s[:, GATE_LORA + 3 * G + 2 * LORA:]

    ones_bd = _ones_bd()
    wlin = _dot(jnp.tanh(wl), w2_ref[...]) + w0_ref[...]
    logw = -jnp.exp(-jax.nn.softplus(-wlin) - 0.5)
    a = jax.nn.sigmoid(_dot(al, a2_ref[...]) + a0_ref[...])

    kraw = k * kk_ref[...]
    nrm = jnp.sqrt(_head_sum(kraw * kraw, ones_bd))
    kkn = kraw / jnp.maximum(nrm, 1e-12)
    ka = ka_ref[...]
    kd_sum = jnp.zeros_like(k)
    for d in range(2):
        a_d = a[:, d * G:(d + 1) * G]
        kd = k * (1.0 + (a_d - 1.0) * ka)
        logw_ref[0, d] = logw[:, d * G:(d + 1) * G]
        b_ref[0, d] = kkn * a_d
        kd_ref[0, d] = kd
        kd_sum = kd_sum + kd
    coef = _head_sum(r * kd_sum * rk_ref[...], ones_bd)
    kkn_ref[0] = kkn
    v_ref[0] = v
    r_ref[0] = r
    bonus_ref[0] = coef * v
    gate_ref[0] = _dot(jax.nn.sigmoid(gl), g2_ref[...])


def _blockdiag2(w):
    z = jnp.zeros_like(w[0])
    return jnp.concatenate([jnp.concatenate([w[0], z], axis=1),
                            jnp.concatenate([z, w[1]], axis=1)], axis=0)


def _rwkv_prep(u, mu_prev, mu_next, w0, w2, a0, a2, g2, k_k, k_a, r_k, tm):
    B, T, C = u.shape
    assert C == RWKV_COLS and T % tm == 0 and tm % 8 == 0
    nt = T // tm
    hb = tm // 8
    row = lambda x: x.reshape(1, -1).astype(F32)
    tok = lambda: pl.BlockSpec((1, tm, GROUP), lambda b, i: (b, i, 0))
    tok2 = lambda: pl.BlockSpec((1, 2, tm, GROUP), lambda b, i: (b, 0, i, 0))
    full = lambda shp: pl.BlockSpec(shp, lambda b, i: (0,) * len(shp))
    sd = lambda: jax.ShapeDtypeStruct((B, T, GROUP), F32)
    sd2 = lambda: jax.ShapeDtypeStruct((B, 2, T, GROUP), F32)
    return pl.pallas_call(
        functools.partial(_rwkv_prep_kernel, tm=tm),
        grid=(B, nt),
        in_specs=[
            pl.BlockSpec((1, tm, C), lambda b, i: (b, i, 0)),
            pl.BlockSpec((1, 8, C), lambda b, i: (b, jnp.maximum(i * hb - 1, 0), 0)),
            pl.BlockSpec((1, 8, C), lambda b, i: (b, jnp.minimum((i + 1) * hb, T // 8 - 1), 0)),
            full((1, C)), full((1, C)),
            full((1, 2 * GROUP)), full((2 * LORA, 2 * GROUP)),
            full((1, 2 * GROUP)), full((2 * LORA, 2 * GROUP)),
            full((GATE_LORA, GROUP)), full((1, GROUP)), full((1, GROUP)), full((1, GROUP)),
        ],
        out_specs=[tok2(), tok(), tok2(), tok2(), tok(), tok(), tok(), tok()],
        out_shape=[sd2(), sd(), sd2(), sd2(), sd(), sd(), sd(), sd()],
        scratch_shapes=[pltpu.VMEM((tm + 16, C), F32)],
        compiler_params=_params("parallel", "parallel"),
        name="rwkv_prep",
    )(u, u, u, row(mu_prev), row(mu_next), row(w0), _blockdiag2(w2).astype(BF16),
      row(a0), _blockdiag2(a2).astype(BF16), g2.astype(BF16), row(k_k), row(k_a), row(r_k))


def _wkv_chunk_kernel(logw_ref, kk_ref, b_ref, kd_ref, v_ref, r_ref,
                      g_ref, n_ref, pc_ref, rp_ref, y0_ref, *, nc):
    d = pl.program_id(1)
    t_w = lax.broadcasted_iota(jnp.int32, (CHUNK, GROUP), 0)
    s_w = lax.broadcasted_iota(jnp.int32, (CHUNK, GROUP), 1) % HEAD
    sign = jnp.where(d == 0, 1, -1)
    ahead = (t_w - s_w) * sign
    strict = ahead > 0
    incl = ahead >= 0
    eye_w = jnp.where(s_w == t_w, 1.0, 0.0)
    diag8 = (t_w // 8) == (s_w // 8)
    offs = [((t_w // (2 * m)) == (s_w // (2 * m))) & ((t_w // m) != (s_w // m)) for m in (8, 16, 32)]
    t_s = lax.broadcasted_iota(jnp.int32, (CHUNK, CHUNK), 0)
    s_s = lax.broadcasted_iota(jnp.int32, (CHUNK, CHUNK), 1)
    tri = jnp.where((t_s - s_s) * sign >= 0, 1.0, 0.0).astype(BF16)

    def body(c, carry):
        rows = pl.ds(pl.multiple_of(c * CHUNK, CHUNK), CHUNK)
        logw = logw_ref[0, 0, rows, :]
        kk = kk_ref[0, rows, :]
        b = b_ref[0, 0, rows, :]
        kd = kd_ref[0, 0, rows, :]
        v = v_ref[0, rows, :]
        r = r_ref[0, rows, :]

        l1, l2, l3 = _split3(logw)
        cs = jnp.dot(tri, jnp.concatenate([l1, l2, l3], axis=1), preferred_element_type=F32)
        lcum = cs[:, :GROUP] + cs[:, GROUP:2 * GROUP] + cs[:, 2 * GROUP:]
        ltot = jnp.sum(logw, axis=0, keepdims=True)
        e_incl = jnp.exp(lcum)
        e_inv = jnp.exp(-lcum)
        e_prev = jnp.exp(lcum - logw)
        e_end = jnp.exp(ltot - lcum)
        a_t = -kk * e_prev
        b_t = b * e_inv
        k_t = kd * e_inv
        r_t = r * e_incl
        b_h = b * e_end
        k_h = kd * e_end

        ar = jnp.concatenate([a_t, r_t], axis=0)
        p1 = _dot_nt(ar, _bd(b_t))
        p2 = _dot_nt(ar, _bd(k_t))
        a_ab = jnp.where(strict, p1[:CHUNK], 0.0)
        a_rb = jnp.where(incl, p1[CHUNK:], 0.0)
        a_ak = jnp.where(strict, p2[:CHUNK], 0.0)
        a_rk = jnp.where(incl, p2[CHUNK:], 0.0)

        l8 = jnp.where(diag8, a_ab, 0.0)
        m = _dot(l8, _bd(l8))
        tinv = eye_w + l8
        tm_ = _dot(jnp.concatenate([tinv, m], axis=0), _bd(m))
        tinv = tinv + tm_[:CHUNK]
        tinv = tinv + _dot(tinv, _bd(tm_[CHUNK:]))
        for off in offs:
            tinv = tinv + _dot(tinv, _bd(_dot(jnp.where(off, a_ab, 0.0), _bd(tinv))))

        x = _dot(a_ak, _bd(v))
        a_p = _dot(tinv, _bd(a_t))
        u0 = _dot(tinv, _bd(x))
        r_p = r_t + _dot(a_rb, _bd(a_p))
        y0 = _dot(a_rb, _bd(u0)) + _dot(a_rk, _bd(v))
        g_full = _dot_tn(a_p, b_h)
        n_full = _dot_tn(jnp.concatenate([u0, v], axis=0), jnp.concatenate([b_h, k_h], axis=0))

        g_ref[0, 0, c] = _compact(g_full)
        n_ref[0, 0, c] = _compact(n_full)
        pc_ref[0, 0, c] = jnp.exp(ltot)
        rp_ref[0, 0, rows, :] = r_p
        y0_ref[0, 0, rows, :] = y0
        return carry

    lax.fori_loop(0, nc, body, 0)


def _wkv_chunk(logw, kk, b, kd, v, r, tm):
    B, _, T, _ = logw.shape
    assert T % tm == 0 and tm % CHUNK == 0
    nt, nc = T // tm, tm // CHUNK
    nct = T // CHUNK
    tok = lambda: pl.BlockSpec((1, tm, GROUP), lambda bb, d, i: (bb, i, 0))
    tok2 = lambda: pl.BlockSpec((1, 1, tm, GROUP), lambda bb, d, i: (bb, d, i, 0))
    mat = lambda: pl.BlockSpec((1, 1, nc, HEAD, GROUP), lambda bb, d, i: (bb, d, i, 0, 0))
    return pl.pallas_call(
        functools.partial(_wkv_chunk_kernel, nc=nc),
        grid=(B, 2, nt),
        in_specs=[tok2(), tok(), tok2(), tok2(), tok(), tok()],
        out_specs=[mat(), mat(),
                   pl.BlockSpec((1, 1, nc, 1, GROUP), lambda bb, d, i: (bb, d, i, 0, 0)),
                   tok2(), tok2()],
        out_shape=[jax.ShapeDtypeStruct((B, 2, nct, HEAD, GROUP), F32),
                   jax.ShapeDtypeStruct((B, 2, nct, HEAD, GROUP), F32),
                   jax.ShapeDtypeStruct((B, 2, nct, 1, GROUP), F32),
                   jax.ShapeDtypeStruct((B, 2, T, GROUP), F32),
                   jax.ShapeDtypeStruct((B, 2, T, GROUP), F32)],
        compiler_params=_params("parallel", "parallel", "parallel"),
        name="wkv_chunk",
    )(logw, kk, b, kd, v, r)


def _wkv_scan_kernel(g_ref, n_ref, pc_ref, s0_ref, sall_ref, send_ref, st_ref, *, nb, cb):
    d = pl.program_id(0)
    i = pl.program_id(1)
    B = st_ref.shape[0]

    @pl.when(i == 0)
    def _():
        st_ref[...] = s0_ref[:, 0]

    def body(jj, carry):
        j = jnp.where(d == 0, jj, cb - 1 - jj)
        for bb in range(B):
            s = st_ref[bb]
            sall_ref[bb, 0, j] = s
            st_ref[bb] = s * pc_ref[bb, 0, j] + _dot(s, _bd(g_ref[bb, 0, j])) + n_ref[bb, 0, j]
        return carry

    lax.fori_loop(0, cb, body, 0)

    @pl.when(i == nb - 1)
    def _():
        send_ref[:, 0] = st_ref[...]


def _wkv_scan(g, n, pc, s0, cb):
    B, _, nct, _, _ = g.shape
    assert nct % cb == 0
    nb = nct // cb
    blk = lambda d, i: jnp.where(d == 0, i, nb - 1 - i)
    mat = lambda: pl.BlockSpec((B, 1, cb, HEAD, GROUP), lambda d, i: (0, d, blk(d, i), 0, 0))
    st = lambda: pl.BlockSpec((B, 1, HEAD, GROUP), lambda d, i: (0, d, 0, 0))
    return pl.pallas_call(
        functools.partial(_wkv_scan_kernel, nb=nb, cb=cb),
        grid=(2, nb),
        in_specs=[mat(), mat(),
                  pl.BlockSpec((B, 1, cb, 1, GROUP), lambda d, i: (0, d, blk(d, i), 0, 0)),
                  st()],
        out_specs=[mat(), st()],
        out_shape=[jax.ShapeDtypeStruct((B, 2, nct, HEAD, GROUP), F32),
                   jax.ShapeDtypeStruct((B, 2, HEAD, GROUP), F32)],
        scratch_shapes=[pltpu.VMEM((B, HEAD, GROUP), F32)],
        compiler_params=_params("arbitrary", "arbitrary"),
        name="wkv_scan",
    )(g, n, pc, s0)


def _wkv_out_kernel(rp_ref, y0_ref, s_ref, bonus_ref, gate_ref, gnw_ref, gnb_ref, o_ref, *, nc):
    ones_bd = _ones_bd()

    def body(c, carry):
        rows = pl.ds(pl.multiple_of(c * CHUNK, CHUNK), CHUNK)
        y = jnp.zeros((CHUNK, GROUP), F32)
        for d in range(2):
            y = y + _dot_nt(rp_ref[0, d, rows, :], _bd(s_ref[0, d, c])) + y0_ref[0, d, rows, :]
        mu = _head_sum(y, ones_bd) * (1.0 / HEAD)
        yc = y - mu
        var = _head_sum(yc * yc, ones_bd) * (1.0 / HEAD)
        yn = yc * lax.rsqrt(var + GN_EPS) * gnw_ref[...] + gnb_ref[...]
        o_ref[0, rows, :] = (yn + bonus_ref[0, rows, :]) * gate_ref[0, rows, :]
        return carry

    lax.fori_loop(0, nc, body, 0)


def _wkv_out(rp, y0, sall, bonus, gate, gn_w, gn_b, tm):
    B, _, T, _ = rp.shape
    nt, nc = T // tm, tm // CHUNK
    tok = lambda: pl.BlockSpec((1, tm, GROUP), lambda bb, i: (bb, i, 0))
    tok2 = lambda: pl.BlockSpec((1, 2, tm, GROUP), lambda bb, i: (bb, 0, i, 0))
    vec = lambda: pl.BlockSpec((1, GROUP), lambda bb, i: (0, 0))
    return pl.pallas_call(
        functools.partial(_wkv_out_kernel, nc=nc),
        grid=(B, nt),
        in_specs=[tok2(), tok2(),
                  pl.BlockSpec((1, 2, nc, HEAD, GROUP), lambda bb, i: (bb, 0, i, 0, 0)),
                  tok(), tok(), vec(), vec()],
        out_specs=tok(),
        out_shape=jax.ShapeDtypeStruct((B, T, GROUP), F32),
        compiler_params=_params("parallel", "parallel"),
        name="wkv_out",
    )(rp, y0, sall, bonus, gate, gn_w.reshape(1, -1), gn_b.reshape(1, -1))


def _rwkv_stream(u, s0, lw, tm_prep, tm_chunk, cb):
    (mu_prev, mu_next, w0, w2, a0, a2, g2, k_k, k_a, r_k, gn_w, gn_b) = lw
    logw, kk, b, kd, v, r, bonus, gate = _rwkv_prep(u, mu_prev, mu_next, w0, w2, a0, a2, g2,
                                                    k_k, k_a, r_k, tm_prep)
    g, n, pc, rp, y0 = _wkv_chunk(logw, kk, b, kd, v, r, tm_chunk)
    sall, send = _wkv_scan(g, n, pc, s0, cb)
    y = _wkv_out(rp, y0, sall, bonus, gate, gn_w, gn_b, tm_chunk)
    return y, send


def _mod_kernel(c_ref, w_ref, b_ref, o_ref):
    cc = c_ref[...]
    o_ref[0] = _dot(cc * jax.nn.sigmoid(cc), w_ref[0]) + b_ref[0]


def _mod(cc, w_mod, b_mod, nb=4):
    L, D, N = w_mod.shape
    R = cc.shape[0]
    bn = N // nb
    return pl.pallas_call(
        _mod_kernel,
        grid=(L, nb),
        in_specs=[pl.BlockSpec((R, D), lambda l, j: (0, 0)),
                  pl.BlockSpec((1, D, bn), lambda l, j: (l, 0, j)),
                  pl.BlockSpec((1, 1, bn), lambda l, j: (l, 0, j))],
        out_specs=pl.BlockSpec((1, R, bn), lambda l, j: (l, 0, j)),
        out_shape=jax.ShapeDtypeStruct((L, R, N), F32),
        compiler_params=_params("parallel", "parallel"),
        name="adaln_mod",
    )(cc, w_mod, b_mod.reshape(L, 1, N))


def _modulated(h, g, shift, scale):
    y = h * lax.rsqrt(jnp.mean(h * h, axis=-1, keepdims=True) + RMS_EPS) * g
    return y * (1.0 + scale) + shift


def _proj_kernel(*refs, widths, has_pos):
    if has_pos:
        x_ref, pos_ref, g_ref, sh_ref, sc_ref, w_ref, h_ref, *outs = refs
        h = x_ref[0] + pos_ref[...]
        h_ref[0] = h
    else:
        x_ref, g_ref, sh_ref, sc_ref, w_ref, *outs = refs
        h = x_ref[0]
    a = _modulated(h, g_ref[...], sh_ref[0], sc_ref[0]).astype(BF16)
    off = 0
    for o_ref, w in zip(outs, widths):
        o_ref[0] = jnp.dot(a, w_ref[:, off:off + w], preferred_element_type=F32)
        off += w


def _proj(h, g, shift, scale, w, widths, tm, pos=None):
    B, T, D = h.shape
    N = w.shape[1]
    assert sum(widths) == N and T % tm == 0
    tokd = pl.BlockSpec((1, tm, D), lambda b, i: (b, i, 0))
    vec = pl.BlockSpec((1, 1, D), lambda b, i: (b, 0, 0))
    in_specs = [tokd]
    args = [h]
    if pos is not None:
        in_specs.append(pl.BlockSpec((tm, D), lambda b, i: (i, 0)))
        args.append(pos)
    in_specs += [pl.BlockSpec((1, D), lambda b, i: (0, 0)), vec, vec,
                 pl.BlockSpec((D, N), lambda b, i: (0, 0))]
    args += [g.reshape(1, D), shift, scale, w]
    out_specs = [pl.BlockSpec((1, tm, wd), lambda b, i: (b, i, 0)) for wd in widths]
    out_shape = [jax.ShapeDtypeStruct((B, T, wd), F32) for wd in widths]
    if pos is not None:
        out_specs = [tokd] + out_specs
        out_shape = [jax.ShapeDtypeStruct((B, T, D), F32)] + out_shape
    return pl.pallas_call(
        functools.partial(_proj_kernel, widths=tuple(widths), has_pos=pos is not None),
        grid=(B, T // tm),
        in_specs=in_specs, out_specs=out_specs, out_shape=out_shape,
        compiler_params=_params("parallel", "parallel"),
        name="norm_proj",
    )(*args)


def _halo_specs(T, tm, C):
    hb = tm // HALO
    return [pl.BlockSpec((1, tm, C), lambda b, i: (b, i, 0)),
            pl.BlockSpec((1, HALO, C), lambda b, i: (b, jnp.maximum(i * hb - 1, 0), 0)),
            pl.BlockSpec((1, HALO, C), lambda b, i: (b, jnp.minimum((i + 1) * hb, T // HALO - 1), 0))]


def _fill_halo(buf_ref, main, prev, nxt, tm):
    i = pl.program_id(1)
    buf_ref[0:HALO, :] = jnp.where(i == 0, 0.0, prev)
    buf_ref[HALO:HALO + tm, :] = main
    buf_ref[HALO + tm:, :] = jnp.where(i == pl.num_programs(1) - 1, 0.0, nxt)


def _pool_kernel(u_ref, up_ref, un_ref, coef_ref, hw_ref, w_ref, sc_ref, o_ref, buf_ref, *, tm, T, rb):
    _fill_halo(buf_ref, u_ref[0], up_ref[0], un_ref[0], tm)
    maxhw = POOL_WINDOWS[-1] // 2
    hw = hw_ref[...]
    for r0 in range(0, tm, rb):
        total = jnp.zeros((rb, GROUP), F32)
        for j in range(-maxhw, maxhw + 1):
            total = total + coef_ref[j + maxhw:j + maxhw + 1, :] * buf_ref[HALO + r0 + j:HALO + r0 + j + rb, :]
        t = pl.program_id(1) * tm + r0 + lax.broadcasted_iota(jnp.int32, (rb, GROUP), 0)
        clip = lambda z: jnp.clip(z, 0, T)
        count = (clip(t + hw) - clip(t - hw)) + (clip(t + hw + 1) - clip(t - hw + 1))
        diff = total / count.astype(F32) - u_ref[0, r0:r0 + rb, :]
        o_ref[0, r0:r0 + rb, :] = _dot(diff, w_ref[...]) * sc_ref[...]


def _bd4_np(blocks):
    n = blocks[0].shape[0]
    out = np.zeros((4 * n, 4 * n), np.float64)
    for h, blk in enumerate(blocks):
        out[h * n:(h + 1) * n, h * n:(h + 1) * n] = blk
    return out


def _bd4(blocks):
    n = blocks.shape[1]
    z = jnp.zeros((n, n), blocks.dtype)
    return jnp.concatenate(
        [jnp.concatenate([blocks[h] if g == h else z for g in range(4)], axis=1) for h in range(4)], axis=0)


def _pool(u, pool_w, pool_scale, tm):
    B, T, C = u.shape
    assert C == GROUP and T % tm == 0
    maxhw = POOL_WINDOWS[-1] // 2
    hw = np.repeat(np.array([w // 2 for w in POOL_WINDOWS], np.int32), GROUP // len(POOL_WINDOWS))
    j = np.arange(-maxhw, maxhw + 1)[:, None]
    coef = np.where(np.abs(j) > hw[None], 0.0, np.where(np.abs(j) == hw[None], 1.0, 2.0)).astype(np.float32)
    full = lambda shp: pl.BlockSpec(shp, lambda b, i: (0,) * len(shp))
    return pl.pallas_call(
        functools.partial(_pool_kernel, tm=tm, T=T, rb=min(tm, 128)),
        grid=(B, T // tm),
        in_specs=_halo_specs(T, tm, C) + [full(coef.shape), full((1, C)), full((C, C)), full((1, C))],
        out_specs=pl.BlockSpec((1, tm, C), lambda b, i: (b, i, 0)),
        out_shape=jax.ShapeDtypeStruct((B, T, C), F32),
        scratch_shapes=[pltpu.VMEM((tm + 2 * HALO, C), F32)],
        compiler_params=_params("parallel", "parallel"),
        name="pool_mixer",
    )(u, u, u, jnp.asarray(coef), jnp.asarray(hw).reshape(1, C), _bd4(pool_w).astype(BF16),
      pool_scale.reshape(1, C))


def _conv_kernel(u_ref, up_ref, un_ref, dw_ref, db_ref, lg_ref, lb_ref, pw_ref, o_ref, buf_ref, *, tm, rb):
    glu = lambda z: z[:, :GROUP] * jax.nn.sigmoid(z[:, GROUP:])
    _fill_halo(buf_ref, glu(u_ref[0]), glu(up_ref[0]), glu(un_ref[0]), tm)
    half = CONV_WIDTH // 2
    for r0 in range(0, tm, rb):
        acc = jnp.zeros((rb, GROUP), F32) + db_ref[...]
        for j in range(CONV_WIDTH):
            lo = HALO + r0 + j - half
            acc = acc + dw_ref[j:j + 1, :] * buf_ref[lo:lo + rb, :]
        mu = jnp.mean(acc, axis=-1, keepdims=True)
        xc = acc - mu
        var = jnp.mean(xc * xc, axis=-1, keepdims=True)
        hn = xc * lax.rsqrt(var + LN_EPS) * lg_ref[...] + lb_ref[...]
        o_ref[0, r0:r0 + rb, :] = _dot(hn * jax.nn.sigmoid(hn), pw_ref[...])


def _conv(u, dw_w, dw_b, ln_g, ln_b, pw, tm):
    B, T, C2 = u.shape
    C = C2 // 2
    assert C == GROUP and T % tm == 0 and HALO >= CONV_WIDTH // 2
    full = lambda shp: pl.BlockSpec(shp, lambda b, i: (0,) * len(shp))
    return pl.pallas_call(
        functools.partial(_conv_kernel, tm=tm, rb=min(tm, 128)),
        grid=(B, T // tm),
        in_specs=_halo_specs(T, tm, C2) + [full((CONV_WIDTH, C)), full((1, C)), full((1, C)), full((1, C)),
                                           full((C, C))],
        out_specs=pl.BlockSpec((1, tm, C), lambda b, i: (b, i, 0)),
        out_shape=jax.ShapeDtypeStruct((B, T, C), F32),
        scratch_shapes=[pltpu.VMEM((tm + 2 * HALO, C), F32)],
        compiler_params=_params("parallel", "parallel"),
        name="conv_module",
    )(u, u, u, dw_w, dw_b.reshape(1, C), ln_g.reshape(1, C), ln_b.reshape(1, C), pw.astype(BF16))


def _np_split(m):
    m = jnp.asarray(np.asarray(m, np.float32))
    hi = m.astype(BF16)
    lo = (m - hi.astype(F32)).astype(BF16)
    return hi, lo


def _dot3(a, b_hi, b_lo):
    a_hi, a_lo = _split2(a)
    return (jnp.dot(a_hi, b_hi, preferred_element_type=F32) + jnp.dot(a_lo, b_hi, preferred_element_type=F32)
            + jnp.dot(a_hi, b_lo, preferred_element_type=F32))


def _dot3c(c_hi, c_lo, a):
    a_hi, a_lo = _split2(a)
    return (jnp.dot(c_hi, a_hi, preferred_element_type=F32) + jnp.dot(c_hi, a_lo, preferred_element_type=F32)
            + jnp.dot(c_lo, a_hi, preferred_element_type=F32))


def _cos_sin(n):
    k = np.arange(n, dtype=np.float64)
    ang = 2.0 * np.pi * np.outer(k, k) / n
    return np.cos(ang), np.sin(ang)


def _chan_dft_np():
    c, s = _cos_sin(HEAD)
    return np.concatenate([_bd4_np([c] * 4), _bd4_np([s] * 4)], axis=1)


def _fourier_s1_kernel(x_ref, twc_ref, tws_ref, csh_ref, csl_ref, m1h_ref, m1l_ref, ore_ref, oim_ref, *, tb):
    for t in range(tb):
        z = x_ref[0, :, t * GROUP:(t + 1) * GROUP]
        w = _dot3(z, csh_ref[...], csl_ref[...])
        zz = jnp.concatenate([w[:, :GROUP], w[:, GROUP:]], axis=0)
        y = _dot3c(m1h_ref[...], m1l_ref[...], zz)
        yre, yim = y[:HEAD], y[HEAD:]
        ct = jnp.concatenate([twc_ref[t], twc_ref[t]], axis=1)
        st = jnp.concatenate([tws_ref[t], tws_ref[t]], axis=1)
        ore_ref[0, t] = yre * ct + yim * st
        oim_ref[0, t] = yim * ct - yre * st


def _fourier_s2_kernel(yre_ref, yim_ref, c2h_ref, c2l_ref, fw_ref, o_ref, *, fb):
    for f in range(fb):
        cols = slice(f * GROUP, (f + 1) * GROUP)
        yy = jnp.concatenate([yre_ref[0, :, cols], yim_ref[0, :, cols]], axis=0)
        res = _dot3c(c2h_ref[...], c2l_ref[...], yy)
        o_ref[0, :, cols] = _dot(res, fw_ref[...])


def _fourier_small_kernel(x_ref, csh_ref, csl_ref, cth_ref, ctl_ref, fw_ref, o_ref):
    w = _dot3(x_ref[0], csh_ref[...], csl_ref[...])
    zz = jnp.concatenate([w[:, :GROUP], w[:, GROUP:]], axis=0)
    o_ref[0] = _dot(_dot3c(cth_ref[...], ctl_ref[...], zz), fw_ref[...])


def _fourier(u, fourier_w, tb=8, fb=8):
    B, T, C = u.shape
    assert C == GROUP
    scale = 1.0 / math.sqrt(T * HEAD)
    csh, csl = _np_split(_chan_dft_np())
    fw = fourier_w.astype(BF16)
    full = lambda shp, n: pl.BlockSpec(shp, lambda *_: (0,) * n)
    if T <= 512:
        ct, st = _cos_sin(T)
        cth, ctl = _np_split(np.concatenate([ct, -st], axis=1) * scale)
        return pl.pallas_call(
            _fourier_small_kernel,
            grid=(B,),
            in_specs=[pl.BlockSpec((1, T, C), lambda b: (b, 0, 0)),
                      full((C, 2 * C), 2), full((C, 2 * C), 2), full((T, 2 * T), 2), full((T, 2 * T), 2),
                      full((C, C), 2)],
            out_specs=pl.BlockSpec((1, T, C), lambda b: (b, 0, 0)),
            out_shape=jax.ShapeDtypeStruct((B, T, C), F32),
            compiler_params=_params("parallel"),
            name="fourier_small",
        )(u, csh, csl, cth, ctl, fw)

    n1 = HEAD
    n2 = T // n1
    assert n1 * n2 == T and n2 % tb == 0 and n1 % fb == 0
    c1, s1 = _cos_sin(n1)
    m1h, m1l = _np_split(np.block([[c1, -s1], [-s1, -c1]]))
    f1 = np.arange(n1, dtype=np.float64)[None, :, None]
    t2 = np.arange(n2, dtype=np.float64)[:, None, None]
    ang = 2.0 * np.pi * f1 * t2 / T * np.ones((1, 1, 128))
    twc = jnp.asarray(np.cos(ang).astype(np.float32))
    tws = jnp.asarray(np.sin(ang).astype(np.float32))
    c2, s2 = _cos_sin(n2)
    c2h, c2l = _np_split(np.concatenate([c2, s2], axis=1) * scale)

    yre, yim = pl.pallas_call(
        functools.partial(_fourier_s1_kernel, tb=tb),
        grid=(n2 // tb, B),
        in_specs=[pl.BlockSpec((1, n1, tb * C), lambda j, b: (b, 0, j)),
                  pl.BlockSpec((tb, n1, 128), lambda j, b: (j, 0, 0)),
                  pl.BlockSpec((tb, n1, 128), lambda j, b: (j, 0, 0)),
                  full((C, 2 * C), 2), full((C, 2 * C), 2), full((2 * n1, 2 * n1), 2), full((2 * n1, 2 * n1), 2)],
        out_specs=[pl.BlockSpec((1, tb, n1, C), lambda j, b: (b, j, 0, 0))] * 2,
        out_shape=[jax.ShapeDtypeStruct((B, n2, n1, C), F32)] * 2,
        compiler_params=_params("parallel", "parallel"),
        name="fourier_stage1",
    )(u.reshape(B, n1, n2 * C), twc, tws, csh, csl, m1h, m1l)

    out = pl.pallas_call(
        functools.partial(_fourier_s2_kernel, fb=fb),
        grid=(B, n1 // fb),
        in_specs=[pl.BlockSpec((1, n2, fb * C), lambda b, j: (b, 0, j)),
                  pl.BlockSpec((1, n2, fb * C), lambda b, j: (b, 0, j)),
                  full((n2, 2 * n2), 2), full((n2, 2 * n2), 2), full((C, C), 2)],
        out_specs=pl.BlockSpec((1, n2, fb * C), lambda b, j: (b, 0, j)),
        out_shape=jax.ShapeDtypeStruct((B, n2, n1 * C), F32),
        compiler_params=_params("parallel", "parallel"),
        name="fourier_stage2",
    )(yre.reshape(B, n2, n1 * C), yim.reshape(B, n2, n1 * C), c2h, c2l, fw)
    return out.reshape(B, T, C)


def _out_proj_kernel(h_ref, gt_ref, y0_ref, y1_ref, y2_ref, y3_ref, w_ref, o_ref):
    acc = None
    for n, y_ref in enumerate((y0_ref, y1_ref, y2_ref, y3_ref)):
        part = jnp.dot(y_ref[0].astype(BF16), w_ref[n * GROUP:(n + 1) * GROUP, :], preferred_element_type=F32)
        acc = part if acc is None else acc + part
    o_ref[0] = h_ref[0] + gt_ref[0] * acc


def _out_proj(h, gate, ys, w, tm):
    B, T, D = h.shape
    tokd = pl.BlockSpec((1, tm, D), lambda b, i: (b, i, 0))
    tokg = pl.BlockSpec((1, tm, GROUP), lambda b, i: (b, i, 0))
    return pl.pallas_call(
        _out_proj_kernel,
        grid=(B, T // tm),
        in_specs=[tokd, pl.BlockSpec((1, 1, D), lambda b, i: (b, 0, 0)), tokg, tokg, tokg, tokg,
                  pl.BlockSpec(w.shape, lambda b, i: (0, 0))],
        out_specs=tokd,
        out_shape=jax.ShapeDtypeStruct((B, T, D), F32),
        compiler_params=_params("parallel", "parallel"),
        name="out_proj",
    )(h, gate, *ys, w)


def _ffn_kernel(*refs, F, fc, final):
    if final:
        h_ref, g_ref, sh_ref, sc_ref, gt_ref, wi_ref, wo_ref, fg_ref, o_ref = refs
    else:
        h_ref, g_ref, sh_ref, sc_ref, gt_ref, wi_ref, wo_ref, o_ref = refs
    h = h_ref[0]
    a = _modulated(h, g_ref[...], sh_ref[0], sc_ref[0]).astype(BF16)
    acc = jnp.zeros(h.shape, F32)
    for j in range(F // fc):
        gate = jnp.dot(a, wi_ref[:, j * fc:(j + 1) * fc], preferred_element_type=F32)
        up = jnp.dot(a, wi_ref[:, F + j * fc:F + (j + 1) * fc], preferred_element_type=F32)
        mid = (gate * jax.nn.sigmoid(gate) * up).astype(BF16)
        acc = acc + jnp.dot(mid, wo_ref[j * fc:(j + 1) * fc, :], preferred_element_type=F32)
    out = h + gt_ref[0] * acc
    if final:
        out = out * lax.rsqrt(jnp.mean(out * out, axis=-1, keepdims=True) + RMS_EPS) * fg_ref[...]
    o_ref[0] = out


def _ffn(h, g, shift, scale, gate, w_in, w_out, tm, final_g=None, fc=256):
    B, T, D = h.shape
    F = w_out.shape[0]
    assert F % fc == 0 and T % tm == 0
    tokd = pl.BlockSpec((1, tm, D), lambda b, i: (b, i, 0))
    vec = pl.BlockSpec((1, 1, D), lambda b, i: (b, 0, 0))
    row = pl.BlockSpec((1, D), lambda b, i: (0, 0))
    in_specs = [tokd, row, vec, vec, vec,
                pl.BlockSpec(w_in.shape, lambda b, i: (0, 0)), pl.BlockSpec(w_out.shape, lambda b, i: (0, 0))]
    args = [h, g.reshape(1, D), shift, scale, gate, w_in, w_out]
    if final_g is not None:
        in_specs.append(row)
        args.append(final_g.reshape(1, D))
    return pl.pallas_call(
        functools.partial(_ffn_kernel, F=F, fc=fc, final=final_g is not None),
        grid=(B, T // tm),
        in_specs=in_specs, out_specs=tokd,
        out_shape=jax.ShapeDtypeStruct((B, T, D), F32),
        compiler_params=_params("parallel", "parallel"),
        name="swiglu_ffn",
    )(*args)


def _pos_embed_2d(rows, dim):
    quarter = dim // 4
    omega = 1.0 / (POS_BASE ** (jnp.arange(quarter, dtype=F32) / quarter))
    row = jnp.repeat(jnp.arange(rows, dtype=F32), GRID_W)
    col = jnp.tile(jnp.arange(GRID_W, dtype=F32), rows)

    def enc(p):
        ang = p[:, None] * omega[None, :]
        return jnp.concatenate([jnp.sin(ang), jnp.cos(ang)], axis=-1)

    return jnp.concatenate([enc(row), enc(col)], axis=-1)


def _mixers(u_pool, u_four, u_conv, lw, tm):
    pool_w, pool_scale, fourier_w, dw_w, dw_b, ln_g, ln_b, pw = lw
    return (_pool(u_pool, pool_w, pool_scale, tm), _fourier(u_four, fourier_w),
            _conv(u_conv, dw_w, dw_b, ln_g, ln_b, pw, tm))


def kernel(x, c, ctx, c_ctx, w_mod, b_mod, norm1_g, norm2_g, w_in, w_out, rwkv_mu_prev, rwkv_mu_next, rwkv_w0, rwkv_w2, rwkv_a0, rwkv_a2, rwkv_g2, rwkv_k_k, rwkv_k_a, rwkv_r_k, rwkv_gn_w, rwkv_gn_b, pool_w, pool_scale, fourier_w, conv_dw_w, conv_dw_b, conv_ln_g, conv_ln_b, conv_pw, ffn_w_in, ffn_w_out, final_norm_g):
    B, T, D = x.shape
    Tc = ctx.shape[1]
    depth = w_mod.shape[0]
    widths = (RWKV_COLS, GROUP, GROUP, 2 * GROUP)
    tm, tmc = 512, Tc

    cc = jnp.concatenate([c, c_ctx[None], jnp.zeros((8 - B - 1, D), F32)], axis=0)
    mod = _mod(cc, w_mod, b_mod)
    pos = _pos_embed_2d(T // GRID_W, D)
    s_zero = jnp.zeros((B, 2, HEAD, GROUP), F32)

    h, hc = x, ctx
    for l in range(depth):
        last = l == depth - 1
        mx = [mod[l, :B, n * D:(n + 1) * D][:, None, :] for n in range(6)]
        mc = [jnp.broadcast_to(mod[l, B, n * D:(n + 1) * D][None, None, :], (B, 1, D)) for n in range(6)]
        w_in_b = w_in[l].astype(BF16)
        w_out_b = w_out[l].astype(BF16)
        ffn_in_b = ffn_w_in[l].astype(BF16)
        ffn_out_b = ffn_w_out[l].astype(BF16)
        rw = (rwkv_mu_prev[l], rwkv_mu_next[l], rwkv_w0[l], rwkv_w2[l], rwkv_a0[l], rwkv_a2[l], rwkv_g2[l],
              rwkv_k_k[l], rwkv_k_a[l], rwkv_r_k[l].reshape(-1), rwkv_gn_w[l], rwkv_gn_b[l])
        ow = (pool_w[l], pool_scale[l], fourier_w[l], conv_dw_w[l], conv_dw_b[l], conv_ln_g[l], conv_ln_b[l],
              conv_pw[l])

        if l == 0:
            h, ux_r, ux_p, ux_f, ux_c = _proj(h, norm1_g[l], mx[0], mx[1], w_in_b, widths, tm, pos=pos)
        else:
            ux_r, ux_p, ux_f, ux_c = _proj(h, norm1_g[l], mx[0], mx[1], w_in_b, widths, tm)
        if last:
            (uc_r,) = _proj(hc, norm1_g[l], mc[0], mc[1], w_in_b[:, :RWKV_COLS], (RWKV_COLS,), tmc)
        else:
            uc_r, uc_p, uc_f, uc_c = _proj(hc, norm1_g[l], mc[0], mc[1], w_in_b, widths, tmc)

        yc_r, s_ctx = _rwkv_stream(uc_r, s_zero, rw, tmc, tmc, Tc // CHUNK)
        yx_r, _ = _rwkv_stream(ux_r, s_ctx, rw, tm, tm, 16)

        yx = (yx_r,) + _mixers(ux_p, ux_f, ux_c, ow, tm)
        h = _out_proj(h, mx[2], yx, w_out_b, tm)
        h = _ffn(h, norm2_g[l], mx[3], mx[4], mx[5], ffn_in_b, ffn_out_b, tm,
                 final_g=final_norm_g if last else None)
        if not last:
            yc = (yc_r,) + _mixers(uc_p, uc_f, uc_c, ow, tmc)
            hc = _out_proj(hc, mc[2], yc, w_out_b, tmc)
            hc = _ffn(hc, norm2_g[l], mc[3], mc[4], mc[5], ffn_in_b, ffn_out_b, tmc)
    return h
```

```python
import functools
import math

import jax
import jax.numpy as jnp
import numpy as np
from jax import lax
from jax.experimental import pallas as pl
from jax.experimental.pallas import tpu as pltpu

F32 = jnp.float32
BF16 = jnp.bfloat16

GROUP = 256
HEAD = 64
NHEAD = GROUP // HEAD
GATE_LORA = 128
LORA = 64
RWKV_COLS = GATE_LORA + 3 * GROUP + 4 * LORA
GRID_W = 64
POOL_WINDOWS = (2, 4, 8, 16)
CONV_WIDTH = 31
RMS_EPS = 1e-6
GN_EPS = 64e-5
LN_EPS = 1e-5
POS_BASE = 10000.0

CHUNK = 64
WKV_UNROLL = 8
HALO = 16
V7X_VMEM_LIMIT = 56 * 1024 * 1024


def _params(*sem):
    return pltpu.CompilerParams(dimension_semantics=sem, vmem_limit_bytes=V7X_VMEM_LIMIT)


def _dot(a, b):
    return jnp.dot(a.astype(BF16), b.astype(BF16), preferred_element_type=F32)


def _dot_nt(a, b):
    return lax.dot_general(a.astype(BF16), b.astype(BF16), (((1,), (1,)), ((), ())),
                           preferred_element_type=F32)


def _dot_tn(a, b):
    return lax.dot_general(a.astype(BF16), b.astype(BF16), (((0,), (0,)), ((), ())),
                           preferred_element_type=F32)


def _split2(x):
    hi = x.astype(BF16)
    lo = (x - hi.astype(F32)).astype(BF16)
    return hi, lo


def _split3(x):
    hi = x.astype(BF16)
    r = x - hi.astype(F32)
    mid = r.astype(BF16)
    lo = (r - mid.astype(F32)).astype(BF16)
    return hi, mid, lo


def _lane_head(shape):
    return lax.broadcasted_iota(jnp.int32, shape, 1) // HEAD


def _bd(xw):
    xb = xw.astype(BF16)
    lh = _lane_head(xb.shape)
    zero = jnp.zeros_like(xb)
    return jnp.concatenate([jnp.where(lh == h, xb, zero) for h in range(NHEAD)], axis=0)


def _compact(full):
    lh = _lane_head((HEAD, GROUP))
    out = jnp.zeros((HEAD, GROUP), F32)
    for h in range(NHEAD):
        out = out + jnp.where(lh == h, full[h * HEAD:(h + 1) * HEAD, :], 0.0)
    return out


def _head_sum(x, ones_bd):
    hi, lo = _split2(x)
    return (jnp.dot(hi, ones_bd, preferred_element_type=F32)
            + jnp.dot(lo, ones_bd, preferred_element_type=F32))


def _ones_bd():
    r = lax.broadcasted_iota(jnp.int32, (GROUP, GROUP), 0) // HEAD
    c = lax.broadcasted_iota(jnp.int32, (GROUP, GROUP), 1) // HEAD
    return jnp.where(r == c, 1.0, 0.0).astype(BF16)


def _rwkv_prep_kernel(u_ref, up_ref, un_ref, mup_ref, mun_ref, w0_ref, w2_ref, a0_ref, a2_ref,
                      g2_ref, kk_ref, ka_ref, rk_ref,
                      logw_ref, kkn_ref, b_ref, kd_ref, v_ref, r_ref, bonus_ref, gate_ref,
                      buf_ref, *, tm):
    i = pl.program_id(1)
    nt = pl.num_programs(1)
    prev_halo = jnp.where(i == 0, 0.0, up_ref[0])
    next_halo = jnp.where(i == nt - 1, 0.0, un_ref[0])
    buf_ref[0:8, :] = prev_halo
    buf_ref[8:8 + tm, :] = u_ref[0]
    buf_ref[8 + tm:16 + tm, :] = next_halo
    u = u_ref[0]
    prev = buf_ref[7:7 + tm, :]
    nxt = buf_ref[9:9 + tm, :]
    s = u + mup_ref[...] * (prev - u) + mun_ref[...] * (nxt - u)

    G = GROUP
    gl = s[:, 0:GATE_LORA]
    r = s[:, GATE_LORA:GATE_LORA + G]
    k = s[:, GATE_LORA + G:GATE_LORA + 2 * G]
    v = s[:, GATE_LORA + 2 * G:GATE_LORA + 3 * G]
    wl = s[:, GATE_LORA + 3 * G:GATE_LORA + 3 * G + 2 * LORA]
    al = s[:, GATE_LORA + 3 * G + 2 * LORA:]

    ones_bd = _ones_bd()
    wlin = _dot(jnp.tanh(wl), w2_ref[...]) + w0_ref[...]
    logw = -jnp.exp(-jax.nn.softplus(-wlin) - 0.5)
    a = jax.nn.sigmoid(_dot(al, a2_ref[...]) + a0_ref[...])

    kraw = k * kk_ref[...]
    nrm = jnp.sqrt(_head_sum(kraw * kraw, ones_bd))
    kkn = kraw / jnp.maximum(nrm, 1e-12)
    ka = ka_ref[...]
    kd_sum = jnp.zeros_like(k)
    for d in range(2):
        a_d = a[:, d * G:(d + 1) * G]
        kd = k * (1.0 + (a_d - 1.0) * ka)
        logw_ref[0, d] = logw[:, d * G:(d + 1) * G]
        b_ref[0, d] = kkn * a_d
        kd_ref[0, d] = kd
        kd_sum = kd_sum + kd
    coef = _head_sum(r * kd_sum * rk_ref[...], ones_bd)
    kkn_ref[0] = kkn
    v_ref[0] = v
    r_ref[0] = r
    bonus_ref[0] = coef * v
    gate_ref[0] = _dot(jax.nn.sigmoid(gl), g2_ref[...])


def _blockdiag2(w):
    z = jnp.zeros_like(w[0])
    return jnp.concatenate([jnp.concatenate([w[0], z], axis=1),
                            jnp.concatenate([z, w[1]], axis=1)], axis=0)


def _rwkv_prep(u, mu_prev, mu_next, w0, w2, a0, a2, g2, k_k, k_a, r_k, tm):
    B, T, C = u.shape
    assert C == RWKV_COLS and T % tm == 0 and tm % 8 == 0
    nt = T // tm
    hb = tm // 8
    row = lambda x: x.reshape(1, -1).astype(F32)
    tok = lambda: pl.BlockSpec((1, tm, GROUP), lambda b, i: (b, i, 0))
    tok2 = lambda: pl.BlockSpec((1, 2, tm, GROUP), lambda b, i: (b, 0, i, 0))
    full = lambda shp: pl.BlockSpec(shp, lambda b, i: (0,) * len(shp))
    sd = lambda: jax.ShapeDtypeStruct((B, T, GROUP), F32)
    sd2 = lambda: jax.ShapeDtypeStruct((B, 2, T, GROUP), F32)
    return pl.pallas_call(
        functools.partial(_rwkv_prep_kernel, tm=tm),
        grid=(B, nt),
        in_specs=[
            pl.BlockSpec((1, tm, C), lambda b, i: (b, i, 0)),
            pl.BlockSpec((1, 8, C), lambda b, i: (b, jnp.maximum(i * hb - 1, 0), 0)),
            pl.BlockSpec((1, 8, C), lambda b, i: (b, jnp.minimum((i + 1) * hb, T // 8 - 1), 0)),
            full((1, C)), full((1, C)),
            full((1, 2 * GROUP)), full((2 * LORA, 2 * GROUP)),
            full((1, 2 * GROUP)), full((2 * LORA, 2 * GROUP)),
            full((GATE_LORA, GROUP)), full((1, GROUP)), full((1, GROUP)), full((1, GROUP)),
        ],
        out_specs=[tok2(), tok(), tok2(), tok2(), tok(), tok(), tok(), tok()],
        out_shape=[sd2(), sd(), sd2(), sd2(), sd(), sd(), sd(), sd()],
        scratch_shapes=[pltpu.VMEM((tm + 16, C), F32)],
        compiler_params=_params("parallel", "parallel"),
        name="rwkv_prep",
    )(u, u, u, row(mu_prev), row(mu_next), row(w0), _blockdiag2(w2).astype(BF16),
      row(a0), _blockdiag2(a2).astype(BF16), g2.astype(BF16), row(k_k), row(k_a), row(r_k))


def _wkv_chunk_kernel(logw_ref, kk_ref, b_ref, kd_ref, v_ref, r_ref,
                      g_ref, n_ref, pc_ref, rp_ref, y0_ref, *, nc, uf):
    d = pl.program_id(1)
    t_w = lax.broadcasted_iota(jnp.int32, (CHUNK, GROUP), 0)
    s_w = lax.broadcasted_iota(jnp.int32, (CHUNK, GROUP), 1) % HEAD
    sign = jnp.where(d == 0, 1, -1)
    ahead = (t_w - s_w) * sign
    strict = ahead > 0
    incl = ahead >= 0
    eye_w = jnp.where(s_w == t_w, 1.0, 0.0)
    diag8 = (t_w // 8) == (s_w // 8)
    offs = [((t_w // (2 * m)) == (s_w // (2 * m))) & ((t_w // m) != (s_w // m)) for m in (8, 16, 32)]
    t_s = lax.broadcasted_iota(jnp.int32, (CHUNK, CHUNK), 0)
    s_s = lax.broadcasted_iota(jnp.int32, (CHUNK, CHUNK), 1)
    tri = jnp.where((t_s - s_s) * sign >= 0, 1.0, 0.0).astype(BF16)

    def each(f, *lists):
        return [f(*args) for args in zip(*lists)]

    def body(cc, carry):
        cs_ = [cc * uf + q for q in range(uf)]
        rows = [pl.ds(pl.multiple_of(c * CHUNK, CHUNK), CHUNK) for c in cs_]
        logw = [logw_ref[0, 0, rw, :] for rw in rows]
        kk = [kk_ref[0, rw, :] for rw in rows]
        b = [b_ref[0, 0, rw, :] for rw in rows]
        kd = [kd_ref[0, 0, rw, :] for rw in rows]
        v = [v_ref[0, rw, :] for rw in rows]
        r = [r_ref[0, rw, :] for rw in rows]

        def cumsum(lw):
            l1, l2, l3 = _split3(lw)
            cs = jnp.dot(tri, jnp.concatenate([l1, l2, l3], axis=1), preferred_element_type=F32)
            return cs[:, :GROUP] + cs[:, GROUP:2 * GROUP] + cs[:, 2 * GROUP:]

        lcum = each(cumsum, logw)
        ltot = each(lambda lw: jnp.sum(lw, axis=0, keepdims=True), logw)
        e_incl = each(jnp.exp, lcum)
        e_inv = each(lambda z: jnp.exp(-z), lcum)
        e_prev = each(lambda z, lw: jnp.exp(z - lw), lcum, logw)
        e_end = each(lambda lt, z: jnp.exp(lt - z), ltot, lcum)
        a_t = each(lambda x_, e: -x_ * e, kk, e_prev)
        b_t = each(jnp.multiply, b, e_inv)
        k_t = each(jnp.multiply, kd, e_inv)
        r_t = each(jnp.multiply, r, e_incl)
        b_h = each(jnp.multiply, b, e_end)
        k_h = each(jnp.multiply, kd, e_end)

        ar = each(lambda x_, y_: jnp.concatenate([x_, y_], axis=0), a_t, r_t)
        p1 = each(lambda x_, y_: _dot_nt(x_, _bd(y_)), ar, b_t)
        p2 = each(lambda x_, y_: _dot_nt(x_, _bd(y_)), ar, k_t)
        a_ab = each(lambda p: jnp.where(strict, p[:CHUNK], 0.0), p1)
        a_rb = each(lambda p: jnp.where(incl, p[CHUNK:], 0.0), p1)
        a_ak = each(lambda p: jnp.where(strict, p[:CHUNK], 0.0), p2)
        a_rk = each(lambda p: jnp.where(incl, p[CHUNK:], 0.0), p2)

        mm = lambda x_, y_: _dot(x_, _bd(y_))
        l8 = each(lambda z: jnp.where(diag8, z, 0.0), a_ab)
        m = each(mm, l8, l8)
        tinv = each(lambda z: eye_w + z, l8)
        tm_ = each(lambda t_, m_: mm(jnp.concatenate([t_, m_], axis=0), m_), tinv, m)
        tinv = each(lambda t_, z: t_ + z[:CHUNK], tinv, tm_)
        tinv = each(lambda t_, z: t_ + mm(t_, z[CHUNK:]), tinv, tm_)
        for off in offs:
            lt_ = each(lambda z, t_: mm(jnp.where(off, z, 0.0), t_), a_ab, tinv)
            tinv = each(lambda t_, z: t_ + mm(t_, z), tinv, lt_)

        x = each(mm, a_ak, v)
        a_p = each(mm, tinv, a_t)
        u0 = each(mm, tinv, x)
        r_p = each(lambda rt, arb, ap: rt + mm(arb, ap), r_t, a_rb, a_p)
        y0 = each(lambda arb, u_, ark, v_: mm(arb, u_) + mm(ark, v_), a_rb, u0, a_rk, v)
        g_c = each(lambda ap, bh: _compact(_dot_tn(ap, bh)), a_p, b_h)
        n_c = each(lambda u_, v_, bh, kh: _compact(_dot_tn(jnp.concatenate([u_, v_], axis=0),
                                                           jnp.concatenate([bh, kh], axis=0))),
                   u0, v, b_h, k_h)
        for q, c in enumerate(cs_):
            g_ref[0, 0, c] = g_c[q]
            n_ref[0, 0, c] = n_c[q]
            pc_ref[0, 0, c] = jnp.exp(ltot[q])
            rp_ref[0, 0, rows[q], :] = r_p[q]
            y0_ref[0, 0, rows[q], :] = y0[q]
        return carry

    lax.fori_loop(0, nc // uf, body, 0)


def _wkv_chunk(logw, kk, b, kd, v, r, tm):
    B, _, T, _ = logw.shape
    assert T % tm == 0 and tm % CHUNK == 0
    nt, nc = T // tm, tm // CHUNK
    nct = T // CHUNK
    tok = lambda: pl.BlockSpec((1, tm, GROUP), lambda bb, d, i: (bb, i, 0))
    tok2 = lambda: pl.BlockSpec((1, 1, tm, GROUP), lambda bb, d, i: (bb, d, i, 0))
    mat = lambda: pl.BlockSpec((1, 1, nc, HEAD, GROUP), lambda bb, d, i: (bb, d, i, 0, 0))
    return pl.pallas_call(
        functools.partial(_wkv_chunk_kernel, nc=nc, uf=math.gcd(nc, WKV_UNROLL)),
        grid=(B, 2, nt),
        in_specs=[tok2(), tok(), tok2(), tok2(), tok(), tok()],
        out_specs=[mat(), mat(),
                   pl.BlockSpec((1, 1, nc, 1, GROUP), lambda bb, d, i: (bb, d, i, 0, 0)),
                   tok2(), tok2()],
        out_shape=[jax.ShapeDtypeStruct((B, 2, nct, HEAD, GROUP), F32),
                   jax.ShapeDtypeStruct((B, 2, nct, HEAD, GROUP), F32),
                   jax.ShapeDtypeStruct((B, 2, nct, 1, GROUP), F32),
                   jax.ShapeDtypeStruct((B, 2, T, GROUP), F32),
                   jax.ShapeDtypeStruct((B, 2, T, GROUP), F32)],
        compiler_params=_params("parallel", "parallel", "parallel"),
        name="wkv_chunk",
    )(logw, kk, b, kd, v, r)


def _wkv_scan_kernel(g_ref, n_ref, pc_ref, s0_ref, sall_ref, send_ref, st_ref, *, nb, cb):
    d = pl.program_id(0)
    i = pl.program_id(1)
    B = st_ref.shape[0]

    @pl.when(i == 0)
    def _():
        st_ref[...] = s0_ref[:, 0]

    def body(jj, carry):
        j = jnp.where(d == 0, jj, cb - 1 - jj)
        for bb in range(B):
            s = st_ref[bb]
            sall_ref[bb, 0, j] = s
            st_ref[bb] = s * pc_ref[bb, 0, j] + _dot(s, _bd(g_ref[bb, 0, j])) + n_ref[bb, 0, j]
        return carry

    lax.fori_loop(0, cb, body, 0)

    @pl.when(i == nb - 1)
    def _():
        send_ref[:, 0] = st_ref[...]


def _wkv_scan(g, n, pc, s0, cb):
    B, _, nct, _, _ = g.shape
    assert nct % cb == 0
    nb = nct // cb
    blk = lambda d, i: jnp.where(d == 0, i, nb - 1 - i)
    mat = lambda: pl.BlockSpec((B, 1, cb, HEAD, GROUP), lambda d, i: (0, d, blk(d, i), 0, 0))
    st = lambda: pl.BlockSpec((B, 1, HEAD, GROUP), lambda d, i: (0, d, 0, 0))
    return pl.pallas_call(
        functools.partial(_wkv_scan_kernel, nb=nb, cb=cb),
        grid=(2, nb),
        in_specs=[mat(), mat(),
                  pl.BlockSpec((B, 1, cb, 1, GROUP), lambda d, i: (0, d, blk(d, i), 0, 0)),
                  st()],
        out_specs=[mat(), st()],
        out_shape=[jax.ShapeDtypeStruct((B, 2, nct, HEAD, GROUP), F32),
                   jax.ShapeDtypeStruct((B, 2, HEAD, GROUP), F32)],
        scratch_shapes=[pltpu.VMEM((B, HEAD, GROUP), F32)],
        compiler_params=_params("arbitrary", "arbitrary"),
        name="wkv_scan",
    )(g, n, pc, s0)


def _wkv_out_kernel(rp_ref, y0_ref, s_ref, bonus_ref, gate_ref, gnw_ref, gnb_ref, o_ref, *, nc, uf, rb):
    ones_bd = _ones_bd()

    def body(cc, carry):
        for q in range(uf):
            c = cc * uf + q
            rows = pl.ds(pl.multiple_of(c * CHUNK, CHUNK), CHUNK)
            y = y0_ref[0, 0, rows, :] + y0_ref[0, 1, rows, :]
            for d in range(2):
                y = y + _dot_nt(rp_ref[0, d, rows, :], _bd(s_ref[0, d, c]))
            o_ref[0, rows, :] = y
        return carry

    lax.fori_loop(0, nc // uf, body, 0)

    for r0 in range(0, nc * CHUNK, rb):
        rows = slice(r0, r0 + rb)
        y = o_ref[0, rows, :]
        mu = _head_sum(y, ones_bd) * (1.0 / HEAD)
        yc = y - mu
        var = _head_sum(yc * yc, ones_bd) * (1.0 / HEAD)
        yn = yc * lax.rsqrt(var + GN_EPS) * gnw_ref[...] + gnb_ref[...]
        o_ref[0, rows, :] = (yn + bonus_ref[0, rows, :]) * gate_ref[0, rows, :]


def _wkv_out(rp, y0, sall, bonus, gate, gn_w, gn_b, tm):
    B, _, T, _ = rp.shape
    nt, nc = T // tm, tm // CHUNK
    tok = lambda: pl.BlockSpec((1, tm, GROUP), lambda bb, i: (bb, i, 0))
    tok2 = lambda: pl.BlockSpec((1, 2, tm, GROUP), lambda bb, i: (bb, 0, i, 0))
    vec = lambda: pl.BlockSpec((1, GROUP), lambda bb, i: (0, 0))
    return pl.pallas_call(
        functools.partial(_wkv_out_kernel, nc=nc, uf=math.gcd(nc, WKV_UNROLL), rb=min(tm, 256)),
        grid=(B, nt),
        in_specs=[tok2(), tok2(),
                  pl.BlockSpec((1, 2, nc, HEAD, GROUP), lambda bb, i: (bb, 0, i, 0, 0)),
                  tok(), tok(), vec(), vec()],
        out_specs=tok(),
        out_shape=jax.ShapeDtypeStruct((B, T, GROUP), F32),
        compiler_params=_params("parallel", "parallel"),
        name="wkv_out",
    )(rp, y0, sall, bonus, gate, gn_w.reshape(1, -1), gn_b.reshape(1, -1))


def _rwkv_stream(u, s0, lw, tm_prep, tm_chunk, cb):
    (mu_prev, mu_next, w0, w2, a0, a2, g2, k_k, k_a, r_k, gn_w, gn_b) = lw
    logw, kk, b, kd, v, r, bonus, gate = _rwkv_prep(u, mu_prev, mu_next, w0, w2, a0, a2, g2,
                                                    k_k, k_a, r_k, tm_prep)
    g, n, pc, rp, y0 = _wkv_chunk(logw, kk, b, kd, v, r, tm_chunk)
    sall, send = _wkv_scan(g, n, pc, s0, cb)
    y = _wkv_out(rp, y0, sall, bonus, gate, gn_w, gn_b, tm_chunk)
    return y, send


def _mod_kernel(c_ref, w_ref, b_ref, o_ref):
    cc = c_ref[...]
    o_ref[0] = _dot(cc * jax.nn.sigmoid(cc), w_ref[0]) + b_ref[0]


def _mod(cc, w_mod, b_mod, nb=4):
    L, D, N = w_mod.shape
    R = cc.shape[0]
    bn = N // nb
    return pl.pallas_call(
        _mod_kernel,
        grid=(L, nb),
        in_specs=[pl.BlockSpec((R, D), lambda l, j: (0, 0)),
                  pl.BlockSpec((1, D, bn), lambda l, j: (l, 0, j)),
                  pl.BlockSpec((1, 1, bn), lambda l, j: (l, 0, j))],
        out_specs=pl.BlockSpec((1, R, bn), lambda l, j: (l, 0, j)),
        out_shape=jax.ShapeDtypeStruct((L, R, N), F32),
        compiler_params=_params("parallel", "parallel"),
        name="adaln_mod",
    )(cc, w_mod, b_mod.reshape(L, 1, N))


def _modulated(h, g, shift, scale):
    y = h * lax.rsqrt(jnp.mean(h * h, axis=-1, keepdims=True) + RMS_EPS) * g
    return y * (1.0 + scale) + shift


def _proj_kernel(*refs, widths, has_pos):
    if has_pos:
        x_ref, pos_ref, g_ref, sh_ref, sc_ref, w_ref, h_ref, *outs = refs
        h = x_ref[0] + pos_ref[...]
        h_ref[0] = h
    else:
        x_ref, g_ref, sh_ref, sc_ref, w_ref, *outs = refs
        h = x_ref[0]
    a = _modulated(h, g_ref[...], sh_ref[0], sc_ref[0]).astype(BF16)
    off = 0
    for o_ref, w in zip(outs, widths):
        o_ref[0] = jnp.dot(a, w_ref[:, off:off + w], preferred_element_type=F32)
        off += w


def _proj(h, g, shift, scale, w, widths, tm, pos=None):
    B, T, D = h.shape
    N = w.shape[1]
    assert sum(widths) == N and T % tm == 0
    tokd = pl.BlockSpec((1, tm, D), lambda b, i: (b, i, 0))
    vec = pl.BlockSpec((1, 1, D), lambda b, i: (b, 0, 0))
    in_specs = [tokd]
    args = [h]
    if pos is not None:
        in_specs.append(pl.BlockSpec((tm, D), lambda b, i: (i, 0)))
        args.append(pos)
    in_specs += [pl.BlockSpec((1, D), lambda b, i: (0, 0)), vec, vec,
                 pl.BlockSpec((D, N), lambda b, i: (0, 0))]
    args += [g.reshape(1, D), shift, scale, w]
    out_specs = [pl.BlockSpec((1, tm, wd), lambda b, i: (b, i, 0)) for wd in widths]
    out_shape = [jax.ShapeDtypeStruct((B, T, wd), F32) for wd in widths]
    if pos is not None:
        out_specs = [tokd] + out_specs
        out_shape = [jax.ShapeDtypeStruct((B, T, D), F32)] + out_shape
    return pl.pallas_call(
        functools.partial(_proj_kernel, widths=tuple(widths), has_pos=pos is not None),
        grid=(B, T // tm),
        in_specs=in_specs, out_specs=out_specs, out_shape=out_shape,
        compiler_params=_params("parallel", "parallel"),
        name="norm_proj",
    )(*args)


def _halo_specs(T, tm, C):
    hb = tm // HALO
    return [pl.BlockSpec((1, tm, C), lambda b, i: (b, i, 0)),
            pl.BlockSpec((1, HALO, C), lambda b, i: (b, jnp.maximum(i * hb - 1, 0), 0)),
            pl.BlockSpec((1, HALO, C), lambda b, i: (b, jnp.minimum((i + 1) * hb, T // HALO - 1), 0))]


def _fill_halo(buf_ref, main, prev, nxt, tm):
    i = pl.program_id(1)
    buf_ref[0:HALO, :] = jnp.where(i == 0, 0.0, prev)
    buf_ref[HALO:HALO + tm, :] = main
    buf_ref[HALO + tm:, :] = jnp.where(i == pl.num_programs(1) - 1, 0.0, nxt)


def _pool_kernel(u_ref, up_ref, un_ref, coef_ref, hw_ref, w_ref, sc_ref, o_ref, buf_ref, *, tm, T, rb):
    _fill_halo(buf_ref, u_ref[0], up_ref[0], un_ref[0], tm)
    maxhw = POOL_WINDOWS[-1] // 2
    hw = hw_ref[...]
    for r0 in range(0, tm, rb):
        total = jnp.zeros((rb, GROUP), F32)
        for j in range(-maxhw, maxhw + 1):
            total = total + coef_ref[j + maxhw:j + maxhw + 1, :] * buf_ref[HALO + r0 + j:HALO + r0 + j + rb, :]
        t = pl.program_id(1) * tm + r0 + lax.broadcasted_iota(jnp.int32, (rb, GROUP), 0)
        clip = lambda z: jnp.clip(z, 0, T)
        count = (clip(t + hw) - clip(t - hw)) + (clip(t + hw + 1) - clip(t - hw + 1))
        diff = total / count.astype(F32) - u_ref[0, r0:r0 + rb, :]
        o_ref[0, r0:r0 + rb, :] = _dot(diff, w_ref[...]) * sc_ref[...]


def _bd4_np(blocks):
    n = blocks[0].shape[0]
    out = np.zeros((4 * n, 4 * n), np.float64)
    for h, blk in enumerate(blocks):
        out[h * n:(h + 1) * n, h * n:(h + 1) * n] = blk
    return out


def _bd4(blocks):
    n = blocks.shape[1]
    z = jnp.zeros((n, n), blocks.dtype)
    return jnp.concatenate(
        [jnp.concatenate([blocks[h] if g == h else z for g in range(4)], axis=1) for h in range(4)], axis=0)


def _pool(u, pool_w, pool_scale, tm):
    B, T, C = u.shape
    assert C == GROUP and T % tm == 0
    maxhw = POOL_WINDOWS[-1] // 2
    hw = np.repeat(np.array([w // 2 for w in POOL_WINDOWS], np.int32), GROUP // len(POOL_WINDOWS))
    j = np.arange(-maxhw, maxhw + 1)[:, None]
    coef = np.where(np.abs(j) > hw[None], 0.0, np.where(np.abs(j) == hw[None], 1.0, 2.0)).astype(np.float32)
    full = lambda shp: pl.BlockSpec(shp, lambda b, i: (0,) * len(shp))
    return pl.pallas_call(
        functools.partial(_pool_kernel, tm=tm, T=T, rb=min(tm, 128)),
        grid=(B, T // tm),
        in_specs=_halo_specs(T, tm, C) + [full(coef.shape), full((1, C)), full((C, C)), full((1, C))],
        out_specs=pl.BlockSpec((1, tm, C), lambda b, i: (b, i, 0)),
        out_shape=jax.ShapeDtypeStruct((B, T, C), F32),
        scratch_shapes=[pltpu.VMEM((tm + 2 * HALO, C), F32)],
        compiler_params=_params("parallel", "parallel"),
        name="pool_mixer",
    )(u, u, u, jnp.asarray(coef), jnp.asarray(hw).reshape(1, C), _bd4(pool_w).astype(BF16),
      pool_scale.reshape(1, C))


def _conv_kernel(u_ref, up_ref, un_ref, dw_ref, db_ref, lg_ref, lb_ref, pw_ref, o_ref, buf_ref, *, tm, rb):
    glu = lambda z: z[:, :GROUP] * jax.nn.sigmoid(z[:, GROUP:])
    _fill_halo(buf_ref, glu(u_ref[0]), glu(up_ref[0]), glu(un_ref[0]), tm)
    half = CONV_WIDTH // 2
    for r0 in range(0, tm, rb):
        acc = jnp.zeros((rb, GROUP), F32) + db_ref[...]
        for j in range(CONV_WIDTH):
            lo = HALO + r0 + j - half
            acc = acc + dw_ref[j:j + 1, :] * buf_ref[lo:lo + rb, :]
        mu = jnp.mean(acc, axis=-1, keepdims=True)
        xc = acc - mu
        var = jnp.mean(xc * xc, axis=-1, keepdims=True)
        hn = xc * lax.rsqrt(var + LN_EPS) * lg_ref[...] + lb_ref[...]
        o_ref[0, r0:r0 + rb, :] = _dot(hn * jax.nn.sigmoid(hn), pw_ref[...])


def _conv(u, dw_w, dw_b, ln_g, ln_b, pw, tm):
    B, T, C2 = u.shape
    C = C2 // 2
    assert C == GROUP and T % tm == 0 and HALO >= CONV_WIDTH // 2
    full = lambda shp: pl.BlockSpec(shp, lambda b, i: (0,) * len(shp))
    return pl.pallas_call(
        functools.partial(_conv_kernel, tm=tm, rb=min(tm, 128)),
        grid=(B, T // tm),
        in_specs=_halo_specs(T, tm, C2) + [full((CONV_WIDTH, C)), full((1, C)), full((1, C)), full((1, C)),
                                           full((C, C))],
        out_specs=pl.BlockSpec((1, tm, C), lambda b, i: (b, i, 0)),
        out_shape=jax.ShapeDtypeStruct((B, T, C), F32),
        scratch_shapes=[pltpu.VMEM((tm + 2 * HALO, C), F32)],
        compiler_params=_params("parallel", "parallel"),
        name="conv_module",
    )(u, u, u, dw_w, dw_b.reshape(1, C), ln_g.reshape(1, C), ln_b.reshape(1, C), pw.astype(BF16))


def _np_split(m):
    m = jnp.asarray(np.asarray(m, np.float32))
    hi = m.astype(BF16)
    lo = (m - hi.astype(F32)).astype(BF16)
    return hi, lo


def _dot3(a, b_hi, b_lo):
    a_hi, a_lo = _split2(a)
    return (jnp.dot(a_hi, b_hi, preferred_element_type=F32) + jnp.dot(a_lo, b_hi, preferred_element_type=F32)
            + jnp.dot(a_hi, b_lo, preferred_element_type=F32))


def _dot3c(c_hi, c_lo, a):
    a_hi, a_lo = _split2(a)
    return (jnp.dot(c_hi, a_hi, preferred_element_type=F32) + jnp.dot(c_hi, a_lo, preferred_element_type=F32)
            + jnp.dot(c_lo, a_hi, preferred_element_type=F32))


def _cos_sin(n):
    k = np.arange(n, dtype=np.float64)
    ang = 2.0 * np.pi * np.outer(k, k) / n
    return np.cos(ang), np.sin(ang)


def _chan_dft_np():
    c, s = _cos_sin(HEAD)
    return np.concatenate([_bd4_np([c] * 4), _bd4_np([s] * 4)], axis=1)


def _fourier_s1_kernel(x_ref, twc_ref, tws_ref, csh_ref, csl_ref, m1h_ref, m1l_ref, ore_ref, oim_ref, *, tb):
    for t in range(tb):
        z = x_ref[0, :, t * GROUP:(t + 1) * GROUP]
        w = _dot3(z, csh_ref[...], csl_ref[...])
        zz = jnp.concatenate([w[:, :GROUP], w[:, GROUP:]], axis=0)
        y = _dot3c(m1h_ref[...], m1l_ref[...], zz)
        yre, yim = y[:HEAD], y[HEAD:]
        ct = jnp.concatenate([twc_ref[t], twc_ref[t]], axis=1)
        st = jnp.concatenate([tws_ref[t], tws_ref[t]], axis=1)
        ore_ref[0, t] = yre * ct + yim * st
        oim_ref[0, t] = yim * ct - yre * st


def _fourier_s2_kernel(yre_ref, yim_ref, c2h_ref, c2l_ref, fw_ref, o_ref, *, fb):
    for f in range(fb):
        cols = slice(f * GROUP, (f + 1) * GROUP)
        yy = jnp.concatenate([yre_ref[0, :, cols], yim_ref[0, :, cols]], axis=0)
        res = _dot3c(c2h_ref[...], c2l_ref[...], yy)
        o_ref[0, :, cols] = _dot(res, fw_ref[...])


def _fourier_small_kernel(x_ref, csh_ref, csl_ref, cth_ref, ctl_ref, fw_ref, o_ref):
    w = _dot3(x_ref[0], csh_ref[...], csl_ref[...])
    zz = jnp.concatenate([w[:, :GROUP], w[:, GROUP:]], axis=0)
    o_ref[0] = _dot(_dot3c(cth_ref[...], ctl_ref[...], zz), fw_ref[...])


def _fourier(u, fourier_w, tb=8, fb=8):
    B, T, C = u.shape
    assert C == GROUP
    scale = 1.0 / math.sqrt(T * HEAD)
    csh, csl = _np_split(_chan_dft_np())
    fw = fourier_w.astype(BF16)
    full = lambda shp, n: pl.BlockSpec(shp, lambda *_: (0,) * n)
    if T <= 512:
        ct, st = _cos_sin(T)
        cth, ctl = _np_split(np.concatenate([ct, -st], axis=1) * scale)
        return pl.pallas_call(
            _fourier_small_kernel,
            grid=(B,),
            in_specs=[pl.BlockSpec((1, T, C), lambda b: (b, 0, 0)),
                      full((C, 2 * C), 2), full((C, 2 * C), 2), full((T, 2 * T), 2), full((T, 2 * T), 2),
                      full((C, C), 2)],
            out_specs=pl.BlockSpec((1, T, C), lambda b: (b, 0, 0)),
            out_shape=jax.ShapeDtypeStruct((B, T, C), F32),
            compiler_params=_params("parallel"),
            name="fourier_small",
        )(u, csh, csl, cth, ctl, fw)

    n1 = HEAD
    n2 = T // n1
    assert n1 * n2 == T and n2 % tb == 0 and n1 % fb == 0
    c1, s1 = _cos_sin(n1)
    m1h, m1l = _np_split(np.block([[c1, -s1], [-s1, -c1]]))
    f1 = np.arange(n1, dtype=np.float64)[None, :, None]
    t2 = np.arange(n2, dtype=np.float64)[:, None, None]
    ang = 2.0 * np.pi * f1 * t2 / T * np.ones((1, 1, 128))
    twc = jnp.asarray(np.cos(ang).astype(np.float32))
    tws = jnp.asarray(np.sin(ang).astype(np.float32))
    c2, s2 = _cos_sin(n2)
    c2h, c2l = _np_split(np.concatenate([c2, s2], axis=1) * scale)

    yre, yim = pl.pallas_call(
        functools.partial(_fourier_s1_kernel, tb=tb),
        grid=(n2 // tb, B),
        in_specs=[pl.BlockSpec((1, n1, tb * C), lambda j, b: (b, 0, j)),
                  pl.BlockSpec((tb, n1, 128), lambda j, b: (j, 0, 0)),
                  pl.BlockSpec((tb, n1, 128), lambda j, b: (j, 0, 0)),
                  full((C, 2 * C), 2), full((C, 2 * C), 2), full((2 * n1, 2 * n1), 2), full((2 * n1, 2 * n1), 2)],
        out_specs=[pl.BlockSpec((1, tb, n1, C), lambda j, b: (b, j, 0, 0))] * 2,
        out_shape=[jax.ShapeDtypeStruct((B, n2, n1, C), F32)] * 2,
        compiler_params=_params("parallel", "parallel"),
        name="fourier_stage1",
    )(u.reshape(B, n1, n2 * C), twc, tws, csh, csl, m1h, m1l)

    out = pl.pallas_call(
        functools.partial(_fourier_s2_kernel, fb=fb),
        grid=(B, n1 // fb),
        in_specs=[pl.BlockSpec((1, n2, fb * C), lambda b, j: (b, 0, j)),
                  pl.BlockSpec((1, n2, fb * C), lambda b, j: (b, 0, j)),
                  full((n2, 2 * n2), 2), full((n2, 2 * n2), 2), full((C, C), 2)],
        out_specs=pl.BlockSpec((1, n2, fb * C), lambda b, j: (b, 0, j)),
        out_shape=jax.ShapeDtypeStruct((B, n2, n1 * C), F32),
        compiler_params=_params("parallel", "parallel"),
        name="fourier_stage2",
    )(yre.reshape(B, n2, n1 * C), yim.reshape(B, n2, n1 * C), c2h, c2l, fw)
    return out.reshape(B, T, C)


def _out_proj_kernel(h_ref, gt_ref, y0_ref, y1_ref, y2_ref, y3_ref, w_ref, o_ref):
    acc = None
    for n, y_ref in enumerate((y0_ref, y1_ref, y2_ref, y3_ref)):
        part = jnp.dot(y_ref[0].astype(BF16), w_ref[n * GROUP:(n + 1) * GROUP, :], preferred_element_type=F32)
        acc = part if acc is None else acc + part
    o_ref[0] = h_ref[0] + gt_ref[0] * acc


def _out_proj(h, gate, ys, w, tm):
    B, T, D = h.shape
    tokd = pl.BlockSpec((1, tm, D), lambda b, i: (b, i, 0))
    tokg = pl.BlockSpec((1, tm, GROUP), lambda b, i: (b, i, 0))
    return pl.pallas_call(
        _out_proj_kernel,
        grid=(B, T // tm),
        in_specs=[tokd, pl.BlockSpec((1, 1, D), lambda b, i: (b, 0, 0)), tokg, tokg, tokg, tokg,
                  pl.BlockSpec(w.shape, lambda b, i: (0, 0))],
        out_specs=tokd,
        out_shape=jax.ShapeDtypeStruct((B, T, D), F32),
        compiler_params=_params("parallel", "parallel"),
        name="out_proj",
    )(h, gate, *ys, w)


def _ffn_kernel(*refs, F, fc, final):
    if final:
        h_ref, g_ref, sh_ref, sc_ref, gt_ref, wi_ref, wo_ref, fg_ref, o_ref = refs
    else:
        h_ref, g_ref, sh_ref, sc_ref, gt_ref, wi_ref, wo_ref, o_ref = refs
    h = h_ref[0]
    a = _modulated(h, g_ref[...], sh_ref[0], sc_ref[0]).astype(BF16)
    acc = jnp.zeros(h.shape, F32)
    for j in range(F // fc):
        gate = jnp.dot(a, wi_ref[:, j * fc:(j + 1) * fc], preferred_element_type=F32)
        up = jnp.dot(a, wi_ref[:, F + j * fc:F + (j + 1) * fc], preferred_element_type=F32)
        mid = (gate * jax.nn.sigmoid(gate) * up).astype(BF16)
        acc = acc + jnp.dot(mid, wo_ref[j * fc:(j + 1) * fc, :], preferred_element_type=F32)
    out = h + gt_ref[0] * acc
    if final:
        out = out * lax.rsqrt(jnp.mean(out * out, axis=-1, keepdims=True) + RMS_EPS) * fg_ref[...]
    o_ref[0] = out


def _ffn(h, g, shift, scale, gate, w_in, w_out, tm, final_g=None, fc=256):
    B, T, D = h.shape
    F = w_out.shape[0]
    assert F % fc == 0 and T % tm == 0
    tokd = pl.BlockSpec((1, tm, D), lambda b, i: (b, i, 0))
    vec = pl.BlockSpec((1, 1, D), lambda b, i: (b, 0, 0))
    row = pl.BlockSpec((1, D), lambda b, i: (0, 0))
    in_specs = [tokd, row, vec, vec, vec,
                pl.BlockSpec(w_in.shape, lambda b, i: (0, 0)), pl.BlockSpec(w_out.shape, lambda b, i: (0, 0))]
    args = [h, g.reshape(1, D), shift, scale, gate, w_in, w_out]
    if final_g is not None:
        in_specs.append(row)
        args.append(final_g.reshape(1, D))
    return pl.pallas_call(
        functools.partial(_ffn_kernel, F=F, fc=fc, final=final_g is not None),
        grid=(B, T // tm),
        in_specs=in_specs, out_specs=tokd,
        out_shape=jax.ShapeDtypeStruct((B, T, D), F32),
        compiler_params=_params("parallel", "parallel"),
        name="swiglu_ffn",
    )(*args)


def _pos_embed_2d(rows, dim):
    quarter = dim // 4
    omega = 1.0 / (POS_BASE ** (jnp.arange(quarter, dtype=F32) / quarter))
    row = jnp.repeat(jnp.arange(rows, dtype=F32), GRID_W)
    col = jnp.tile(jnp.arange(GRID_W, dtype=F32), rows)

    def enc(p):
        ang = p[:, None] * omega[None, :]
        return jnp.concatenate([jnp.sin(ang), jnp.cos(ang)], axis=-1)

    return jnp.concatenate([enc(row), enc(col)], axis=-1)


def _mixers(u_pool, u_four, u_conv, lw, tm):
    pool_w, pool_scale, fourier_w, dw_w, dw_b, ln_g, ln_b, pw = lw
    return (_pool(u_pool, pool_w, pool_scale, tm), _fourier(u_four, fourier_w),
            _conv(u_conv, dw_w, dw_b, ln_g, ln_b, pw, tm))


def kernel(x, c, ctx, c_ctx, w_mod, b_mod, norm1_g, norm2_g, w_in, w_out, rwkv_mu_prev, rwkv_mu_next, rwkv_w0, rwkv_w2, rwkv_a0, rwkv_a2, rwkv_g2, rwkv_k_k, rwkv_k_a, rwkv_r_k, rwkv_gn_w, rwkv_gn_b, pool_w, pool_scale, fourier_w, conv_dw_w, conv_dw_b, conv_ln_g, conv_ln_b, conv_pw, ffn_w_in, ffn_w_out, final_norm_g):
    B, T, D = x.shape
    Tc = ctx.shape[1]
    depth = w_mod.shape[0]
    widths = (RWKV_COLS, GROUP, GROUP, 2 * GROUP)
    tm, tmc = 512, Tc

    cc = jnp.concatenate([c, c_ctx[None], jnp.zeros((8 - B - 1, D), F32)], axis=0)
    mod = _mod(cc, w_mod, b_mod)
    pos = _pos_embed_2d(T // GRID_W, D)
    s_zero = jnp.zeros((B, 2, HEAD, GROUP), F32)

    h, hc = x, ctx
    for l in range(depth):
        last = l == depth - 1
        mx = [mod[l, :B, n * D:(n + 1) * D][:, None, :] for n in range(6)]
        mc = [jnp.broadcast_to(mod[l, B, n * D:(n + 1) * D][None, None, :], (B, 1, D)) for n in range(6)]
        w_in_b = w_in[l].astype(BF16)
        w_out_b = w_out[l].astype(BF16)
        ffn_in_b = ffn_w_in[l].astype(BF16)
        ffn_out_b = ffn_w_out[l].astype(BF16)
        rw = (rwkv_mu_prev[l], rwkv_mu_next[l], rwkv_w0[l], rwkv_w2[l], rwkv_a0[l], rwkv_a2[l], rwkv_g2[l],
              rwkv_k_k[l], rwkv_k_a[l], rwkv_r_k[l].reshape(-1), rwkv_gn_w[l], rwkv_gn_b[l])
        ow = (pool_w[l], pool_scale[l], fourier_w[l], conv_dw_w[l], conv_dw_b[l], conv_ln_g[l], conv_ln_b[l],
              conv_pw[l])

        if l == 0:
            h, ux_r, ux_p, ux_f, ux_c = _proj(h, norm1_g[l], mx[0], mx[1], w_in_b, widths, tm, pos=pos)
        else:
            ux_r, ux_p, ux_f, ux_c = _proj(h, norm1_g[l], mx[0], mx[1], w_in_b, widths, tm)
        if last:
            (uc_r,) = _proj(hc, norm1_g[l], mc[0], mc[1], w_in_b[:, :RWKV_COLS], (RWKV_COLS,), tmc)
        else:
            uc_r, uc_p, uc_f, uc_c = _proj(hc, norm1_g[l], mc[0], mc[1], w_in_b, widths, tmc)

        yc_r, s_ctx = _rwkv_stream(uc_r, s_zero, rw, tmc, tmc, Tc // CHUNK)
        yx_r, _ = _rwkv_stream(ux_r, s_ctx, rw, tm, tm, 16)

        yx = (yx_r,) + _mixers(ux_p, ux_f, ux_c, ow, tm)
        h = _out_proj(h, mx[2], yx, w_out_b, tm)
        h = _ffn(h, norm2_g[l], mx[3], mx[4], mx[5], ffn_in_b, ffn_out_b, tm,
                 final_g=final_norm_g if last else None)
        if not last:
            yc = (yc_r,) + _mixers(uc_p, uc_f, uc_c, ow, tmc)
            hc = _out_proj(hc, mc[2], yc, w_out_b, tmc)
            hc = _ffn(hc, norm2_g[l], mc[3], mc[4], mc[5], ffn_in_b, ffn_out_b, tmc)
    return h
```

```python
import functools
import math

import jax
import jax.numpy as jnp
import numpy as np
from jax import lax
from jax.experimental import pallas as pl
from jax.experimental.pallas import tpu as pltpu

F32 = jnp.float32
BF16 = jnp.bfloat16

GROUP = 256
HEAD = 64
NHEAD = GROUP // HEAD
GATE_LORA = 128
LORA = 64
RWKV_COLS = GATE_LORA + 3 * GROUP + 4 * LORA
GRID_W = 64
POOL_WINDOWS = (2, 4, 8, 16)
CONV_WIDTH = 31
RMS_EPS = 1e-6
GN_EPS = 64e-5
LN_EPS = 1e-5
POS_BASE = 10000.0

CHUNK = 64
WKV_UNROLL = 8
HALO = 16
POOL_HALO = 32
V7X_VMEM_LIMIT = 56 * 1024 * 1024


def _params(*sem):
    return pltpu.CompilerParams(dimension_semantics=sem, vmem_limit_bytes=V7X_VMEM_LIMIT)


def _dot(a, b):
    return jnp.dot(a.astype(BF16), b.astype(BF16), preferred_element_type=F32)


def _dot_nt(a, b):
    return lax.dot_general(a.astype(BF16), b.astype(BF16), (((1,), (1,)), ((), ())),
                           preferred_element_type=F32)


def _dot_tn(a, b):
    return lax.dot_general(a.astype(BF16), b.astype(BF16), (((0,), (0,)), ((), ())),
                           preferred_element_type=F32)


def _split2(x):
    hi = x.astype(BF16)
    lo = (x - hi.astype(F32)).astype(BF16)
    return hi, lo


def _split3(x):
    hi = x.astype(BF16)
    r = x - hi.astype(F32)
    mid = r.astype(BF16)
    lo = (r - mid.astype(F32)).astype(BF16)
    return hi, mid, lo


def _lane_head(shape):
    return lax.broadcasted_iota(jnp.int32, shape, 1) // HEAD


def _bd(xw):
    xb = xw.astype(BF16)
    lh = _lane_head(xb.shape)
    zero = jnp.zeros_like(xb)
    return jnp.concatenate([jnp.where(lh == h, xb, zero) for h in range(NHEAD)], axis=0)


def _compact(full):
    lh = _lane_head((HEAD, GROUP))
    out = jnp.zeros((HEAD, GROUP), F32)
    for h in range(NHEAD):
        out = out + jnp.where(lh == h, full[h * HEAD:(h + 1) * HEAD, :], 0.0)
    return out


def _head_sum(x, ones_bd):
    hi, lo = _split2(x)
    return (jnp.dot(hi, ones_bd, preferred_element_type=F32)
            + jnp.dot(lo, ones_bd, preferred_element_type=F32))


def _ones_bd():
    r = lax.broadcasted_iota(jnp.int32, (GROUP, GROUP), 0) // HEAD
    c = lax.broadcasted_iota(jnp.int32, (GROUP, GROUP), 1) // HEAD
    return jnp.where(r == c, 1.0, 0.0).astype(BF16)


def _rwkv_prep_kernel(u_ref, up_ref, un_ref, mup_ref, mun_ref, w0_ref, w2_ref, a0_ref, a2_ref,
                      g2_ref, kk_ref, ka_ref, rk_ref,
                      logw_ref, kkn_ref, b_ref, kd_ref, v_ref, r_ref, bonus_ref, gate_ref, *, tm):
    i = pl.program_id(1)
    nt = pl.num_programs(1)
    u = u_ref[0]
    row = lax.broadcasted_iota(jnp.int32, u.shape, 0)
    prev_row = jnp.where(i == 0, 0.0, up_ref[0, 7:8, :])
    next_row = jnp.where(i == nt - 1, 0.0, un_ref[0, 0:1, :])
    prev = jnp.where(row == 0, prev_row, pltpu.roll(u, 1, 0))
    nxt = jnp.where(row == tm - 1, next_row, pltpu.roll(u, tm - 1, 0))
    s = u + mup_ref[...] * (prev - u) + mun_ref[...] * (nxt - u)

    G = GROUP
    gl = s[:, 0:GATE_LORA]
    r = s[:, GATE_LORA:GATE_LORA + G]
    k = s[:, GATE_LORA + G:GATE_LORA + 2 * G]
    v = s[:, GATE_LORA + 2 * G:GATE_LORA + 3 * G]
    wl = s[:, GATE_LORA + 3 * G:GATE_LORA + 3 * G + 2 * LORA]
    al = s[:, GATE_LORA + 3 * G + 2 * LORA:]

    ones_bd = _ones_bd()
    wlin = _dot(jnp.tanh(wl), w2_ref[...]) + w0_ref[...]
    logw = -jnp.exp(-jax.nn.softplus(-wlin) - 0.5)
    a = jax.nn.sigmoid(_dot(al, a2_ref[...]) + a0_ref[...])

    kraw = k * kk_ref[...]
    nrm = jnp.sqrt(_head_sum(kraw * kraw, ones_bd))
    kkn = kraw / jnp.maximum(nrm, 1e-12)
    ka = ka_ref[...]
    kd_sum = jnp.zeros_like(k)
    for d in range(2):
        a_d = a[:, d * G:(d + 1) * G]
        kd = k * (1.0 + (a_d - 1.0) * ka)
        logw_ref[0, d] = logw[:, d * G:(d + 1) * G]
        b_ref[0, d] = kkn * a_d
        kd_ref[0, d] = kd
        kd_sum = kd_sum + kd
    coef = _head_sum(r * kd_sum * rk_ref[...], ones_bd)
    kkn_ref[0] = kkn
    v_ref[0] = v
    r_ref[0] = r
    bonus_ref[0] = coef * v
    gate_ref[0] = _dot(jax.nn.sigmoid(gl), g2_ref[...])


def _blockdiag2(w):
    z = jnp.zeros_like(w[0])
    return jnp.concatenate([jnp.concatenate([w[0], z], axis=1),
                            jnp.concatenate([z, w[1]], axis=1)], axis=0)


def _rwkv_prep(u, mu_prev, mu_next, w0, w2, a0, a2, g2, k_k, k_a, r_k, tm):
    B, T, C = u.shape
    assert C == RWKV_COLS and T % tm == 0 and tm % 8 == 0
    nt = T // tm
    hb = tm // 8
    row = lambda x: x.reshape(1, -1).astype(F32)
    tok = lambda: pl.BlockSpec((1, tm, GROUP), lambda b, i: (b, i, 0))
    tok2 = lambda: pl.BlockSpec((1, 2, tm, GROUP), lambda b, i: (b, 0, i, 0))
    full = lambda shp: pl.BlockSpec(shp, lambda b, i: (0,) * len(shp))
    sd = lambda: jax.ShapeDtypeStruct((B, T, GROUP), F32)
    sd2 = lambda: jax.ShapeDtypeStruct((B, 2, T, GROUP), F32)
    return pl.pallas_call(
        functools.partial(_rwkv_prep_kernel, tm=tm),
        grid=(B, nt),
        in_specs=[
            pl.BlockSpec((1, tm, C), lambda b, i: (b, i, 0)),
            pl.BlockSpec((1, 8, C), lambda b, i: (b, jnp.maximum(i * hb - 1, 0), 0)),
            pl.BlockSpec((1, 8, C), lambda b, i: (b, jnp.minimum((i + 1) * hb, T // 8 - 1), 0)),
            full((1, C)), full((1, C)),
            full((1, 2 * GROUP)), full((2 * LORA, 2 * GROUP)),
            full((1, 2 * GROUP)), full((2 * LORA, 2 * GROUP)),
            full((GATE_LORA, GROUP)), full((1, GROUP)), full((1, GROUP)), full((1, GROUP)),
        ],
        out_specs=[tok2(), tok(), tok2(), tok2(), tok(), tok(), tok(), tok()],
        out_shape=[sd2(), sd(), sd2(), sd2(), sd(), sd(), sd(), sd()],
        compiler_params=_params("parallel", "parallel"),
        name="rwkv_prep",
    )(u, u, u, row(mu_prev), row(mu_next), row(w0), _blockdiag2(w2).astype(BF16),
      row(a0), _blockdiag2(a2).astype(BF16), g2.astype(BF16), row(k_k), row(k_a), row(r_k))


def _wkv_chunk_kernel(logw_ref, kk_ref, b_ref, kd_ref, v_ref, r_ref,
                      g_ref, n_ref, pc_ref, rp_ref, y0_ref, *, nc, uf):
    d = pl.program_id(1)
    t_w = lax.broadcasted_iota(jnp.int32, (CHUNK, GROUP), 0)
    s_w = lax.broadcasted_iota(jnp.int32, (CHUNK, GROUP), 1) % HEAD
    sign = jnp.where(d == 0, 1, -1)
    ahead = (t_w - s_w) * sign
    strict = ahead > 0
    incl = ahead >= 0
    eye_w = jnp.where(s_w == t_w, 1.0, 0.0)
    diag8 = (t_w // 8) == (s_w // 8)
    offs = [((t_w // (2 * m)) == (s_w // (2 * m))) & ((t_w // m) != (s_w // m)) for m in (8, 16, 32)]
    t_s = lax.broadcasted_iota(jnp.int32, (CHUNK, CHUNK), 0)
    s_s = lax.broadcasted_iota(jnp.int32, (CHUNK, CHUNK), 1)
    tri = jnp.where((t_s - s_s) * sign >= 0, 1.0, 0.0).astype(BF16)

    def each(f, *lists):
        return [f(*args) for args in zip(*lists)]

    def body(cc, carry):
        cs_ = [cc * uf + q for q in range(uf)]
        rows = [pl.ds(pl.multiple_of(c * CHUNK, CHUNK), CHUNK) for c in cs_]
        logw = [logw_ref[0, 0, rw, :] for rw in rows]
        kk = [kk_ref[0, rw, :] for rw in rows]
        b = [b_ref[0, 0, rw, :] for rw in rows]
        kd = [kd_ref[0, 0, rw, :] for rw in rows]
        v = [v_ref[0, rw, :] for rw in rows]
        r = [r_ref[0, rw, :] for rw in rows]

        def cumsum(lw):
            l1, l2, l3 = _split3(lw)
            cs = jnp.dot(tri, jnp.concatenate([l1, l2, l3], axis=1), preferred_element_type=F32)
            return cs[:, :GROUP] + cs[:, GROUP:2 * GROUP] + cs[:, 2 * GROUP:]

        lcum = each(cumsum, logw)
        ltot = each(lambda lw: jnp.sum(lw, axis=0, keepdims=True), logw)
        e_incl = each(jnp.exp, lcum)
        e_inv = each(lambda z: jnp.exp(-z), lcum)
        e_prev = each(lambda z, lw: jnp.exp(z - lw), lcum, logw)
        e_end = each(lambda lt, z: jnp.exp(lt - z), ltot, lcum)
        a_t = each(lambda x_, e: -x_ * e, kk, e_prev)
        b_t = each(jnp.multiply, b, e_inv)
        k_t = each(jnp.multiply, kd, e_inv)
        r_t = each(jnp.multiply, r, e_incl)
        b_h = each(jnp.multiply, b, e_end)
        k_h = each(jnp.multiply, kd, e_end)

        ar = each(lambda x_, y_: jnp.concatenate([x_, y_], axis=0), a_t, r_t)
        p1 = each(lambda x_, y_: _dot_nt(x_, _bd(y_)), ar, b_t)
        p2 = each(lambda x_, y_: _dot_nt(x_, _bd(y_)), ar, k_t)
        a_ab = each(lambda p: jnp.where(strict, p[:CHUNK], 0.0), p1)
        a_rb = each(lambda p: jnp.where(incl, p[CHUNK:], 0.0), p1)
        a_ak = each(lambda p: jnp.where(strict, p[:CHUNK], 0.0), p2)
        a_rk = each(lambda p: jnp.where(incl, p[CHUNK:], 0.0), p2)

        mm = lambda x_, y_: _dot(x_, _bd(y_))
        l8 = each(lambda z: jnp.where(diag8, z, 0.0), a_ab)
        m = each(mm, l8, l8)
        tinv = each(lambda z: eye_w + z, l8)
        tm_ = each(lambda t_, m_: mm(jnp.concatenate([t_, m_], axis=0), m_), tinv, m)
        tinv = each(lambda t_, z: t_ + z[:CHUNK], tinv, tm_)
        tinv = each(lambda t_, z: t_ + mm(t_, z[CHUNK:]), tinv, tm_)
        for off in offs:
            lt_ = each(lambda z, t_: mm(jnp.where(off, z, 0.0), t_), a_ab, tinv)
            tinv = each(lambda t_, z: t_ + mm(t_, z), tinv, lt_)

        xv = each(lambda ak, rk, v_: mm(jnp.concatenate([ak, rk], axis=0), v_), a_ak, a_rk, v)
        a_p = each(mm, tinv, a_t)
        u0 = each(lambda t_, z: mm(t_, z[:CHUNK]), tinv, xv)
        r_p = each(lambda rt, arb, ap: rt + mm(arb, ap), r_t, a_rb, a_p)
        y0 = each(lambda arb, u_, z: mm(arb, u_) + z[CHUNK:], a_rb, u0, xv)
        g_c = each(lambda ap, bh: _compact(_dot_tn(ap, bh)), a_p, b_h)
        n_c = each(lambda u_, v_, bh, kh: _compact(_dot_tn(jnp.concatenate([u_, v_], axis=0),
                                                           jnp.concatenate([bh, kh], axis=0))),
                   u0, v, b_h, k_h)
        for q, c in enumerate(cs_):
            g_ref[0, 0, c] = g_c[q]
            n_ref[0, 0, c] = n_c[q]
            pc_ref[0, 0, c] = jnp.exp(ltot[q])
            rp_ref[0, 0, rows[q], :] = r_p[q]
            y0_ref[0, 0, rows[q], :] = y0[q]
        return carry

    lax.fori_loop(0, nc // uf, body, 0)


def _wkv_chunk(logw, kk, b, kd, v, r, tm):
    B, _, T, _ = logw.shape
    assert T % tm == 0 and tm % CHUNK == 0
    nt, nc = T // tm, tm // CHUNK
    nct = T // CHUNK
    tok = lambda: pl.BlockSpec((1, tm, GROUP), lambda bb, d, i: (bb, i, 0))
    tok2 = lambda: pl.BlockSpec((1, 1, tm, GROUP), lambda bb, d, i: (bb, d, i, 0))
    mat = lambda: pl.BlockSpec((1, 1, nc, HEAD, GROUP), lambda bb, d, i: (bb, d, i, 0, 0))
    return pl.pallas_call(
        functools.partial(_wkv_chunk_kernel, nc=nc, uf=math.gcd(nc, WKV_UNROLL)),
        grid=(B, 2, nt),
        in_specs=[tok2(), tok(), tok2(), tok2(), tok(), tok()],
        out_specs=[mat(), mat(),
                   pl.BlockSpec((1, 1, nc, 1, GROUP), lambda bb, d, i: (bb, d, i, 0, 0)),
                   tok2(), tok2()],
        out_shape=[jax.ShapeDtypeStruct((B, 2, nct, HEAD, GROUP), F32),
                   jax.ShapeDtypeStruct((B, 2, nct, HEAD, GROUP), F32),
                   jax.ShapeDtypeStruct((B, 2, nct, 1, GROUP), F32),
                   jax.ShapeDtypeStruct((B, 2, T, GROUP), F32),
                   jax.ShapeDtypeStruct((B, 2, T, GROUP), F32)],
        compiler_params=_params("parallel", "parallel", "parallel"),
        name="wkv_chunk",
    )(logw, kk, b, kd, v, r)


def _wkv_scan_kernel(gf_ref, gb_ref, nf_ref, nb_ref, pf_ref, pb_ref, s0_ref, sf_ref, sb_ref, send_ref, st_ref,
                     *, nblk, cb):
    i = pl.program_id(0)
    B = st_ref.shape[1]

    @pl.when(i == 0)
    def _():
        for d in range(2):
            st_ref[d] = s0_ref[:, d]

    def body(jj, carry):
        chains = [(d, bb, jj if d == 0 else cb - 1 - jj, g, n, p, o)
                  for d, (g, n, p, o) in enumerate(((gf_ref, nf_ref, pf_ref, sf_ref),
                                                    (gb_ref, nb_ref, pb_ref, sb_ref)))
                  for bb in range(B)]
        s = [st_ref[d, bb] for d, bb, *_ in chains]
        gbd = [_bd(g[bb, 0, j]) for _, bb, j, g, *_ in chains]
        prod = [_dot(s_, g_) for s_, g_ in zip(s, gbd)]
        for s_, pr, (d, bb, j, g, n, p, o) in zip(s, prod, chains):
            o[bb, j] = s_
            st_ref[d, bb] = s_ * p[bb, 0, j] + pr + n[bb, 0, j]
        return carry

    lax.fori_loop(0, cb, body, 0)

    @pl.when(i == nblk - 1)
    def _():
        for d in range(2):
            send_ref[:, d] = st_ref[d]


def _wkv_scan(g, n, pc, s0, cb):
    B, _, nct, _, _ = g.shape
    assert nct % cb == 0
    nblk = nct // cb
    fwd = lambda shp: pl.BlockSpec((B, 1) + shp, lambda i: (0, 0, i, 0, 0))
    bwd = lambda shp: pl.BlockSpec((B, 1) + shp, lambda i: (0, 1, nblk - 1 - i, 0, 0))
    mat, vec = (cb, HEAD, GROUP), (cb, 1, GROUP)
    st = pl.BlockSpec((B, 2, HEAD, GROUP), lambda i: (0, 0, 0, 0))
    return pl.pallas_call(
        functools.partial(_wkv_scan_kernel, nblk=nblk, cb=cb),
        grid=(nblk,),
        in_specs=[fwd(mat), bwd(mat), fwd(mat), bwd(mat), fwd(vec), bwd(vec), st],
        out_specs=[pl.BlockSpec((B,) + mat, lambda i: (0, i, 0, 0)),
                   pl.BlockSpec((B,) + mat, lambda i: (0, nblk - 1 - i, 0, 0)), st],
        out_shape=[jax.ShapeDtypeStruct((B, nct, HEAD, GROUP), F32),
                   jax.ShapeDtypeStruct((B, nct, HEAD, GROUP), F32),
                   jax.ShapeDtypeStruct((B, 2, HEAD, GROUP), F32)],
        scratch_shapes=[pltpu.VMEM((2, B, HEAD, GROUP), F32)],
        compiler_params=_params("arbitrary"),
        name="wkv_scan",
    )(g, g, n, n, pc, pc, s0)


def _wkv_out_kernel(rp_ref, y0_ref, sf_ref, sb_ref, bonus_ref, gate_ref, gnw_ref, gnb_ref, o_ref, *, nc, uf, rb):
    ones_bd = _ones_bd()

    def body(cc, carry):
        for q in range(uf):
            c = cc * uf + q
            rows = pl.ds(pl.multiple_of(c * CHUNK, CHUNK), CHUNK)
            y = y0_ref[0, 0, rows, :] + y0_ref[0, 1, rows, :]
            for d, s_ref in enumerate((sf_ref, sb_ref)):
                y = y + _dot_nt(rp_ref[0, d, rows, :], _bd(s_ref[0, c]))
            o_ref[0, rows, :] = y
        return carry

    lax.fori_loop(0, nc // uf, body, 0)

    for r0 in range(0, nc * CHUNK, rb):
        rows = slice(r0, r0 + rb)
        y = o_ref[0, rows, :]
        mu = _head_sum(y, ones_bd) * (1.0 / HEAD)
        yc = y - mu
        var = _head_sum(yc * yc, ones_bd) * (1.0 / HEAD)
        yn = yc * lax.rsqrt(var + GN_EPS) * gnw_ref[...] + gnb_ref[...]
        o_ref[0, rows, :] = (yn + bonus_ref[0, rows, :]) * gate_ref[0, rows, :]


def _wkv_out(rp, y0, s_f, s_b, bonus, gate, gn_w, gn_b, tm):
    B, _, T, _ = rp.shape
    nt, nc = T // tm, tm // CHUNK
    tok = lambda: pl.BlockSpec((1, tm, GROUP), lambda bb, i: (bb, i, 0))
    tok2 = lambda: pl.BlockSpec((1, 2, tm, GROUP), lambda bb, i: (bb, 0, i, 0))
    vec = lambda: pl.BlockSpec((1, GROUP), lambda bb, i: (0, 0))
    return pl.pallas_call(
        functools.partial(_wkv_out_kernel, nc=nc, uf=math.gcd(nc, WKV_UNROLL), rb=min(tm, 256)),
        grid=(B, nt),
        in_specs=[tok2(), tok2(),
                  pl.BlockSpec((1, nc, HEAD, GROUP), lambda bb, i: (bb, i, 0, 0)),
                  pl.BlockSpec((1, nc, HEAD, GROUP), lambda bb, i: (bb, i, 0, 0)),
                  tok(), tok(), vec(), vec()],
        out_specs=tok(),
        out_shape=jax.ShapeDtypeStruct((B, T, GROUP), F32),
        compiler_params=_params("parallel", "parallel"),
        name="wkv_out",
    )(rp, y0, s_f, s_b, bonus, gate, gn_w.reshape(1, -1), gn_b.reshape(1, -1))


def _rwkv_stream(u, s0, lw, tm_prep, tm_chunk, cb):
    (mu_prev, mu_next, w0, w2, a0, a2, g2, k_k, k_a, r_k, gn_w, gn_b) = lw
    logw, kk, b, kd, v, r, bonus, gate = _rwkv_prep(u, mu_prev, mu_next, w0, w2, a0, a2, g2,
                                                    k_k, k_a, r_k, tm_prep)
    g, n, pc, rp, y0 = _wkv_chunk(logw, kk, b, kd, v, r, tm_chunk)
    s_f, s_b, send = _wkv_scan(g, n, pc, s0, cb)
    y = _wkv_out(rp, y0, s_f, s_b, bonus, gate, gn_w, gn_b, tm_chunk)
    return y, send


def _mod_kernel(c_ref, w_ref, b_ref, o_ref):
    cc = c_ref[...]
    o_ref[0] = _dot(cc * jax.nn.sigmoid(cc), w_ref[0]) + b_ref[0]


def _mod(cc, w_mod, b_mod, nb=4):
    L, D, N = w_mod.shape
    R = cc.shape[0]
    bn = N // nb
    return pl.pallas_call(
        _mod_kernel,
        grid=(L, nb),
        in_specs=[pl.BlockSpec((R, D), lambda l, j: (0, 0)),
                  pl.BlockSpec((1, D, bn), lambda l, j: (l, 0, j)),
                  pl.BlockSpec((1, 1, bn), lambda l, j: (l, 0, j))],
        out_specs=pl.BlockSpec((1, R, bn), lambda l, j: (l, 0, j)),
        out_shape=jax.ShapeDtypeStruct((L, R, N), F32),
        compiler_params=_params("parallel", "parallel"),
        name="adaln_mod",
    )(cc, w_mod, b_mod.reshape(L, 1, N))


def _modulated(h, g, shift, scale):
    y = h * lax.rsqrt(jnp.mean(h * h, axis=-1, keepdims=True) + RMS_EPS) * g
    return y * (1.0 + scale) + shift


def _proj_kernel(*refs, widths, has_pos):
    if has_pos:
        x_ref, rowtab_ref, coltab_ref, g_ref, sh_ref, sc_ref, w_ref, h_ref, *outs = refs
        half = coltab_ref.shape[1]
        for r in range(rowtab_ref.shape[0]):
            rows = slice(r * GRID_W, (r + 1) * GRID_W)
            h_ref[0, rows, :half] = x_ref[0, rows, :half] + rowtab_ref[r:r + 1, :]
            h_ref[0, rows, half:] = x_ref[0, rows, half:] + coltab_ref[...]
        h = h_ref[0]
    else:
        x_ref, g_ref, sh_ref, sc_ref, w_ref, *outs = refs
        h = x_ref[0]
    a = _modulated(h, g_ref[...], sh_ref[0], sc_ref[0]).astype(BF16)
    off = 0
    for o_ref, w in zip(outs, widths):
        o_ref[0] = jnp.dot(a, w_ref[:, off:off + w], preferred_element_type=F32)
        off += w


def _proj(h, g, shift, scale, w, widths, tm, pos=None):
    B, T, D = h.shape
    N = w.shape[1]
    assert sum(widths) == N and T % tm == 0
    tokd = pl.BlockSpec((1, tm, D), lambda b, i: (b, i, 0))
    vec = pl.BlockSpec((1, 1, D), lambda b, i: (b, 0, 0))
    in_specs = [tokd]
    args = [h]
    if pos is not None:
        rowtab, coltab = pos
        assert tm % GRID_W == 0 and (tm // GRID_W) % 8 == 0
        in_specs += [pl.BlockSpec((tm // GRID_W, D // 2), lambda b, i: (i, 0)),
                     pl.BlockSpec((GRID_W, D // 2), lambda b, i: (0, 0))]
        args += [rowtab, coltab]
    in_specs += [pl.BlockSpec((1, D), lambda b, i: (0, 0)), vec, vec,
                 pl.BlockSpec((D, N), lambda b, i: (0, 0))]
    args += [g.reshape(1, D), shift, scale, w]
    out_specs = [pl.BlockSpec((1, tm, wd), lambda b, i: (b, i, 0)) for wd in widths]
    out_shape = [jax.ShapeDtypeStruct((B, T, wd), F32) for wd in widths]
    if pos is not None:
        out_specs = [tokd] + out_specs
        out_shape = [jax.ShapeDtypeStruct((B, T, D), F32)] + out_shape
    return pl.pallas_call(
        functools.partial(_proj_kernel, widths=tuple(widths), has_pos=pos is not None),
        grid=(B, T // tm),
        in_specs=in_specs, out_specs=out_specs, out_shape=out_shape,
        compiler_params=_params("parallel", "parallel"),
        name="norm_proj",
    )(*args)


def _halo_specs(T, tm, C):
    hb = tm // HALO
    return [pl.BlockSpec((1, tm, C), lambda b, i: (b, i, 0)),
            pl.BlockSpec((1, HALO, C), lambda b, i: (b, jnp.maximum(i * hb - 1, 0), 0)),
            pl.BlockSpec((1, HALO, C), lambda b, i: (b, jnp.minimum((i + 1) * hb, T // HALO - 1), 0))]


def _pool_kernel(u_ref, up_ref, un_ref, hw_ref, w_ref, sc_ref, o_ref, x_ref, s2_ref, s4_ref, s8_ref, s16_ref,
                 *, tm, T):
    i = pl.program_id(1)
    P = POOL_HALO
    x_ref[0:P, :] = jnp.where(i == 0, 0.0, up_ref[0])
    x_ref[P:P + tm, :] = u_ref[0]
    x_ref[P + tm:, :] = jnp.where(i == pl.num_programs(1) - 1, 0.0, un_ref[0])
    n2, n4, n8, n16 = tm + 56, tm + 48, tm + 40, tm + 32
    s2_ref[...] = x_ref[0:n2, :] + x_ref[1:n2 + 1, :]
    s4_ref[...] = s2_ref[0:n4, :] + s2_ref[2:n4 + 2, :]
    s8_ref[...] = s4_ref[0:n8, :] + s4_ref[4:n8 + 4, :]
    s16_ref[...] = s8_ref[0:n16, :] + s8_ref[8:n16 + 8, :]
    lane = lax.broadcasted_iota(jnp.int32, (tm, GROUP), 1) // (GROUP // len(POOL_WINDOWS))
    pair = lambda ref, hw: ref[P - hw:P - hw + tm, :] + ref[P - hw + 1:P - hw + 1 + tm, :]
    total = jnp.where(lane == 0, pair(s2_ref, 1),
                      jnp.where(lane == 1, pair(s4_ref, 2),
                                jnp.where(lane == 2, pair(s8_ref, 4), pair(s16_ref, 8))))
    hw = hw_ref[...]
    t = i * tm + lax.broadcasted_iota(jnp.int32, (tm, GROUP), 0)
    clip = lambda z: jnp.clip(z, 0, T)
    count = (clip(t + hw) - clip(t - hw)) + (clip(t + hw + 1) - clip(t - hw + 1))
    diff = total / count.astype(F32) - u_ref[0]
    o_ref[0] = _dot(diff, w_ref[...]) * sc_ref[...]


def _bd4_np(blocks):
    n = blocks[0].shape[0]
    out = np.zeros((4 * n, 4 * n), np.float64)
    for h, blk in enumerate(blocks):
        out[h * n:(h + 1) * n, h * n:(h + 1) * n] = blk
    return out


def _bd4(blocks):
    n = blocks.shape[1]
    z = jnp.zeros((n, n), blocks.dtype)
    return jnp.concatenate(
        [jnp.concatenate([blocks[h] if g == h else z for g in range(4)], axis=1) for h in range(4)], axis=0)


def _pool(u, pool_w, pool_scale, tm):
    B, T, C = u.shape
    P = POOL_HALO
    assert C == GROUP and T % tm == 0 and tm % P == 0 and POOL_WINDOWS == (2, 4, 8, 16)
    hw = np.repeat(np.array([w // 2 for w in POOL_WINDOWS], np.int32), GROUP // len(POOL_WINDOWS))
    full = lambda shp: pl.BlockSpec(shp, lambda b, i: (0,) * len(shp))
    hb = tm // P
    return pl.pallas_call(
        functools.partial(_pool_kernel, tm=tm, T=T),
        grid=(B, T // tm),
        in_specs=[pl.BlockSpec((1, tm, C), lambda b, i: (b, i, 0)),
                  pl.BlockSpec((1, P, C), lambda b, i: (b, jnp.maximum(i * hb - 1, 0), 0)),
                  pl.BlockSpec((1, P, C), lambda b, i: (b, jnp.minimum((i + 1) * hb, T // P - 1), 0)),
                  full((1, C)), full((C, C)), full((1, C))],
        out_specs=pl.BlockSpec((1, tm, C), lambda b, i: (b, i, 0)),
        out_shape=jax.ShapeDtypeStruct((B, T, C), F32),
        scratch_shapes=[pltpu.VMEM((tm + 2 * P, C), F32), pltpu.VMEM((tm + 56, C), F32),
                        pltpu.VMEM((tm + 48, C), F32), pltpu.VMEM((tm + 40, C), F32),
                        pltpu.VMEM((tm + 32, C), F32)],
        compiler_params=_params("parallel", "parallel"),
        name="pool_mixer",
    )(u, u, u, jnp.asarray(hw).reshape(1, C), _bd4(pool_w).astype(BF16), pool_scale.reshape(1, C))


def _conv_kernel(u_ref, up_ref, un_ref, dw_ref, db_ref, lg_ref, lb_ref, pw_ref, o_ref, buf_ref, *, tm, rb):
    glu = lambda z: z[:, :GROUP] * jax.nn.sigmoid(z[:, GROUP:])
    i = pl.program_id(1)
    n = tm + 2 * HALO
    buf_ref[0, 0:HALO, :] = jnp.where(i == 0, 0.0, glu(up_ref[0]))
    buf_ref[0, HALO:HALO + tm, :] = glu(u_ref[0])
    buf_ref[0, HALO + tm:n, :] = jnp.where(i == pl.num_programs(1) - 1, 0.0, glu(un_ref[0]))
    buf_ref[0, n:n + 8, :] = jnp.zeros((8, GROUP), F32)
    for q in range(1, 8):
        buf_ref[q, 0:n, :] = buf_ref[0, q:q + n, :]
    half = CONV_WIDTH // 2
    for r0 in range(0, tm, rb):
        acc = jnp.zeros((rb, GROUP), F32) + db_ref[...]
        for j in range(CONV_WIDTH):
            lo = HALO + r0 + j - half
            acc = acc + dw_ref[j:j + 1, :] * buf_ref[lo % 8, lo - lo % 8:lo - lo % 8 + rb, :]
        mu = jnp.mean(acc, axis=-1, keepdims=True)
        xc = acc - mu
        var = jnp.mean(xc * xc, axis=-1, keepdims=True)
        hn = xc * lax.rsqrt(var + LN_EPS) * lg_ref[...] + lb_ref[...]
        o_ref[0, r0:r0 + rb, :] = _dot(hn * jax.nn.sigmoid(hn), pw_ref[...])


def _conv(u, dw_w, dw_b, ln_g, ln_b, pw, tm):
    B, T, C2 = u.shape
    C = C2 // 2
    assert C == GROUP and T % tm == 0 and HALO >= CONV_WIDTH // 2
    full = lambda shp: pl.BlockSpec(shp, lambda b, i: (0,) * len(shp))
    return pl.pallas_call(
        functools.partial(_conv_kernel, tm=tm, rb=min(tm, 128)),
        grid=(B, T // tm),
        in_specs=_halo_specs(T, tm, C2) + [full((CONV_WIDTH, C)), full((1, C)), full((1, C)), full((1, C)),
                                           full((C, C))],
        out_specs=pl.BlockSpec((1, tm, C), lambda b, i: (b, i, 0)),
        out_shape=jax.ShapeDtypeStruct((B, T, C), F32),
        scratch_shapes=[pltpu.VMEM((8, tm + 2 * HALO + 8, C), F32)],
        compiler_params=_params("parallel", "parallel"),
        name="conv_module",
    )(u, u, u, dw_w, dw_b.reshape(1, C), ln_g.reshape(1, C), ln_b.reshape(1, C), pw.astype(BF16))


def _np_split(m):
    m = jnp.asarray(np.asarray(m, np.float32))
    hi = m.astype(BF16)
    lo = (m - hi.astype(F32)).astype(BF16)
    return hi, lo


def _dot3(a, b_hi, b_lo):
    a_hi, a_lo = _split2(a)
    return (jnp.dot(a_hi, b_hi, preferred_element_type=F32) + jnp.dot(a_lo, b_hi, preferred_element_type=F32)
            + jnp.dot(a_hi, b_lo, preferred_element_type=F32))


def _dot3c(c_hi, c_lo, a):
    a_hi, a_lo = _split2(a)
    return (jnp.dot(c_hi, a_hi, preferred_element_type=F32) + jnp.dot(c_hi, a_lo, preferred_element_type=F32)
            + jnp.dot(c_lo, a_hi, preferred_element_type=F32))


def _cos_sin(n):
    k = np.arange(n, dtype=np.float64)
    ang = 2.0 * np.pi * np.outer(k, k) / n
    return np.cos(ang), np.sin(ang)


def _chan_dft_np():
    c, s = _cos_sin(HEAD)
    return np.concatenate([_bd4_np([c] * 4), _bd4_np([s] * 4)], axis=1)


def _fourier_s1_kernel(x_ref, twc_ref, tws_ref, csh_ref, csl_ref, m1h_ref, m1l_ref, ore_ref, oim_ref, *, tb):
    for t in range(tb):
        z = x_ref[0, :, t, :]
        w = _dot3(z, csh_ref[...], csl_ref[...])
        zz = jnp.concatenate([w[:, :GROUP], w[:, GROUP:]], axis=0)
        y = _dot3c(m1h_ref[...], m1l_ref[...], zz)
        yre, yim = y[:HEAD], y[HEAD:]
        ct = jnp.concatenate([twc_ref[t], twc_ref[t]], axis=1)
        st = jnp.concatenate([tws_ref[t], tws_ref[t]], axis=1)
        ore_ref[0, t] = yre * ct + yim * st
        oim_ref[0, t] = yim * ct - yre * st


def _fourier_s2_kernel(yre_ref, yim_ref, c2h_ref, c2l_ref, fw_ref, o_ref, *, fb):
    for f in range(fb):
        yy = jnp.concatenate([yre_ref[0, :, f, :], yim_ref[0, :, f, :]], axis=0)
        res = _dot3c(c2h_ref[...], c2l_ref[...], yy)
        o_ref[0, :, f, :] = _dot(res, fw_ref[...])


def _fourier_small_kernel(x_ref, csh_ref, csl_ref, cth_ref, ctl_ref, fw_ref, o_ref):
    w = _dot3(x_ref[0], csh_ref[...], csl_ref[...])
    zz = jnp.concatenate([w[:, :GROUP], w[:, GROUP:]], axis=0)
    o_ref[0] = _dot(_dot3c(cth_ref[...], ctl_ref[...], zz), fw_ref[...])


def _fourier(u, fourier_w, tb=8, fb=8):
    B, T, C = u.shape
    assert C == GROUP
    scale = 1.0 / math.sqrt(T * HEAD)
    csh, csl = _np_split(_chan_dft_np())
    fw = fourier_w.astype(BF16)
    full = lambda shp, n: pl.BlockSpec(shp, lambda *_: (0,) * n)
    if T <= 512:
        ct, st = _cos_sin(T)
        cth, ctl = _np_split(np.concatenate([ct, -st], axis=1) * scale)
        return pl.pallas_call(
            _fourier_small_kernel,
            grid=(B,),
            in_specs=[pl.BlockSpec((1, T, C), lambda b: (b, 0, 0)),
                      full((C, 2 * C), 2), full((C, 2 * C), 2), full((T, 2 * T), 2), full((T, 2 * T), 2),
                      full((C, C), 2)],
            out_specs=pl.BlockSpec((1, T, C), lambda b: (b, 0, 0)),
            out_shape=jax.ShapeDtypeStruct((B, T, C), F32),
            compiler_params=_params("parallel"),
            name="fourier_small",
        )(u, csh, csl, cth, ctl, fw)

    n1 = HEAD
    n2 = T // n1
    assert n1 * n2 == T and n2 % tb == 0 and n1 % fb == 0
    c1, s1 = _cos_sin(n1)
    m1h, m1l = _np_split(np.block([[c1, -s1], [-s1, -c1]]))
    f1 = np.arange(n1, dtype=np.float64)[None, :, None]
    t2 = np.arange(n2, dtype=np.float64)[:, None, None]
    ang = 2.0 * np.pi * f1 * t2 / T * np.ones((1, 1, 128))
    twc = jnp.asarray(np.cos(ang).astype(np.float32))
    tws = jnp.asarray(np.sin(ang).astype(np.float32))
    c2, s2 = _cos_sin(n2)
    c2h, c2l = _np_split(np.concatenate([c2, s2], axis=1) * scale)

    yre, yim = pl.pallas_call(
        functools.partial(_fourier_s1_kernel, tb=tb),
        grid=(n2 // tb, B),
        in_specs=[pl.BlockSpec((1, n1, tb, C), lambda j, b: (b, 0, j, 0)),
                  pl.BlockSpec((tb, n1, 128), lambda j, b: (j, 0, 0)),
                  pl.BlockSpec((tb, n1, 128), lambda j, b: (j, 0, 0)),
                  full((C, 2 * C), 2), full((C, 2 * C), 2), full((2 * n1, 2 * n1), 2), full((2 * n1, 2 * n1), 2)],
        out_specs=[pl.BlockSpec((1, tb, n1, C), lambda j, b: (b, j, 0, 0))] * 2,
        out_shape=[jax.ShapeDtypeStruct((B, n2, n1, C), F32)] * 2,
        compiler_params=_params("parallel", "parallel"),
        name="fourier_stage1",
    )(u.reshape(B, n1, n2, C), twc, tws, csh, csl, m1h, m1l)

    out = pl.pallas_call(
        functools.partial(_fourier_s2_kernel, fb=fb),
        grid=(B, n1 // fb),
        in_specs=[pl.BlockSpec((1, n2, fb, C), lambda b, j: (b, 0, j, 0)),
                  pl.BlockSpec((1, n2, fb, C), lambda b, j: (b, 0, j, 0)),
                  full((n2, 2 * n2), 2), full((n2, 2 * n2), 2), full((C, C), 2)],
        out_specs=pl.BlockSpec((1, n2, fb, C), lambda b, j: (b, 0, j, 0)),
        out_shape=jax.ShapeDtypeStruct((B, n2, n1, C), F32),
        compiler_params=_params("parallel", "parallel"),
        name="fourier_stage2",
    )(yre, yim, c2h, c2l, fw)
    return out.reshape(B, T, C)


def _tail_kernel(*refs, F, fc, final):
    (h_ref, y0_ref, y1_ref, y2_ref, y3_ref, wo_ref, gt1_ref, g_ref, sh_ref, sc_ref, gt2_ref,
     wi_ref, wf_ref) = refs[:13]
    o_ref = refs[-1]
    mix = None
    for n, y_ref in enumerate((y0_ref, y1_ref, y2_ref, y3_ref)):
        part = jnp.dot(y_ref[0].astype(BF16), wo_ref[n * GROUP:(n + 1) * GROUP, :], preferred_element_type=F32)
        mix = part if mix is None else mix + part
    h = h_ref[0] + gt1_ref[0] * mix
    a = _modulated(h, g_ref[...], sh_ref[0], sc_ref[0]).astype(BF16)
    acc = jnp.zeros(h.shape, F32)
    for j in range(F // fc):
        gate = jnp.dot(a, wi_ref[:, j * fc:(j + 1) * fc], preferred_element_type=F32)
        up = jnp.dot(a, wi_ref[:, F + j * fc:F + (j + 1) * fc], preferred_element_type=F32)
        mid = (gate * jax.nn.sigmoid(gate) * up).astype(BF16)
        acc = acc + jnp.dot(mid, wf_ref[j * fc:(j + 1) * fc, :], preferred_element_type=F32)
    out = h + gt2_ref[0] * acc
    if final:
        out = out * lax.rsqrt(jnp.mean(out * out, axis=-1, keepdims=True) + RMS_EPS) * refs[13][...]
    o_ref[0] = out


def _tail(h, ys, w_o, gt1, g, shift, scale, gt2, w_in, w_out, tm, final_g=None, fc=256):
    B, T, D = h.shape
    F = w_out.shape[0]
    assert F % fc == 0 and T % tm == 0
    tokd = pl.BlockSpec((1, tm, D), lambda b, i: (b, i, 0))
    tokg = pl.BlockSpec((1, tm, GROUP), lambda b, i: (b, i, 0))
    vec = pl.BlockSpec((1, 1, D), lambda b, i: (b, 0, 0))
    row = pl.BlockSpec((1, D), lambda b, i: (0, 0))
    const = lambda w: pl.BlockSpec(w.shape, lambda b, i: (0, 0), pipeline_mode=pl.Buffered(1))
    in_specs = [tokd, tokg, tokg, tokg, tokg, const(w_o), vec, row, vec, vec, vec, const(w_in), const(w_out)]
    args = [h, *ys, w_o, gt1, g.reshape(1, D), shift, scale, gt2, w_in, w_out]
    if final_g is not None:
        in_specs.append(row)
        args.append(final_g.reshape(1, D))
    return pl.pallas_call(
        functools.partial(_tail_kernel, F=F, fc=fc, final=final_g is not None),
        grid=(B, T // tm),
        in_specs=in_specs, out_specs=tokd,
        out_shape=jax.ShapeDtypeStruct((B, T, D), F32),
        compiler_params=_params("parallel", "parallel"),
        name="layer_tail",
    )(*args)


def _pos_tables(rows, dim):
    quarter = dim // 4
    omega = 1.0 / (POS_BASE ** (jnp.arange(quarter, dtype=F32) / quarter))

    def enc(p):
        ang = p[:, None] * omega[None, :]
        return jnp.concatenate([jnp.sin(ang), jnp.cos(ang)], axis=-1)

    return enc(jnp.arange(rows, dtype=F32)), enc(jnp.arange(GRID_W, dtype=F32))


def _mixers(u_pool, u_four, u_conv, lw, tm):
    pool_w, pool_scale, fourier_w, dw_w, dw_b, ln_g, ln_b, pw = lw
    return (_pool(u_pool, pool_w, pool_scale, tm), _fourier(u_four, fourier_w),
            _conv(u_conv, dw_w, dw_b, ln_g, ln_b, pw, tm))


def kernel(x, c, ctx, c_ctx, w_mod, b_mod, norm1_g, norm2_g, w_in, w_out, rwkv_mu_prev, rwkv_mu_next, rwkv_w0, rwkv_w2, rwkv_a0, rwkv_a2, rwkv_g2, rwkv_k_k, rwkv_k_a, rwkv_r_k, rwkv_gn_w, rwkv_gn_b, pool_w, pool_scale, fourier_w, conv_dw_w, conv_dw_b, conv_ln_g, conv_ln_b, conv_pw, ffn_w_in, ffn_w_out, final_norm_g):
    B, T, D = x.shape
    Tc = ctx.shape[1]
    depth = w_mod.shape[0]
    widths = (RWKV_COLS, GROUP, GROUP, 2 * GROUP)
    tm, tmc = 512, Tc

    cc = jnp.concatenate([c, c_ctx[None], jnp.zeros((8 - B - 1, D), F32)], axis=0)
    mod = _mod(cc, w_mod, b_mod)
    pos = _pos_tables(T // GRID_W, D)
    s_zero = jnp.zeros((B, 2, HEAD, GROUP), F32)

    h, hc = x, ctx
    for l in range(depth):
        last = l == depth - 1
        mx = [mod[l, :B, n * D:(n + 1) * D][:, None, :] for n in range(6)]
        mc = [jnp.broadcast_to(mod[l, B, n * D:(n + 1) * D][None, None, :], (B, 1, D)) for n in range(6)]
        w_in_b = w_in[l].astype(BF16)
        w_out_b = w_out[l].astype(BF16)
        ffn_in_b = ffn_w_in[l].astype(BF16)
        ffn_out_b = ffn_w_out[l].astype(BF16)
        rw = (rwkv_mu_prev[l], rwkv_mu_next[l], rwkv_w0[l], rwkv_w2[l], rwkv_a0[l], rwkv_a2[l], rwkv_g2[l],
              rwkv_k_k[l], rwkv_k_a[l], rwkv_r_k[l].reshape(-1), rwkv_gn_w[l], rwkv_gn_b[l])
        ow = (pool_w[l], pool_scale[l], fourier_w[l], conv_dw_w[l], conv_dw_b[l], conv_ln_g[l], conv_ln_b[l],
              conv_pw[l])

        if l == 0:
            h, ux_r, ux_p, ux_f, ux_c = _proj(h, norm1_g[l], mx[0], mx[1], w_in_b, widths, tm, pos=pos)
        else:
            ux_r, ux_p, ux_f, ux_c = _proj(h, norm1_g[l], mx[0], mx[1], w_in_b, widths, tm)
        if last:
            (uc_r,) = _proj(hc, norm1_g[l], mc[0], mc[1], w_in_b[:, :RWKV_COLS], (RWKV_COLS,), tmc)
        else:
            uc_r, uc_p, uc_f, uc_c = _proj(hc, norm1_g[l], mc[0], mc[1], w_in_b, widths, tmc)

        yc_r, s_ctx = _rwkv_stream(uc_r, s_zero, rw, tmc, tmc, Tc // CHUNK)
        yx_r, _ = _rwkv_stream(ux_r, s_ctx, rw, tm, tm, 16)

        yx = (yx_r,) + _mixers(ux_p, ux_f, ux_c, ow, tm)
        h = _tail(h, yx, w_out_b, mx[2], norm2_g[l], mx[3], mx[4], mx[5], ffn_in_b, ffn_out_b, tm,
                  final_g=final_norm_g if last else None)
        if not last:
            yc = (yc_r,) + _mixers(uc_p, uc_f, uc_c, ow, tmc)
            hc = _tail(hc, yc, w_out_b, mc[2], norm2_g[l], mc[3], mc[4], mc[5], ffn_in_b, ffn_out_b, tmc)
    return h
```

```python
import functools
import math

import jax
import jax.numpy as jnp
import numpy as np
from jax import lax
from jax.experimental import pallas as pl
from jax.experimental.pallas import tpu as pltpu

F32 = jnp.float32
BF16 = jnp.bfloat16

GROUP = 256
HEAD = 64
NHEAD = GROUP // HEAD
GATE_LORA = 128
LORA = 64
RWKV_COLS = GATE_LORA + 3 * GROUP + 4 * LORA
GRID_W = 64
POOL_WINDOWS = (2, 4, 8, 16)
CONV_WIDTH = 31
RMS_EPS = 1e-6
GN_EPS = 64e-5
LN_EPS = 1e-5
POS_BASE = 10000.0

CHUNK = 64
WKV_UNROLL = 8
HALO = 16
POOL_HALO = 32
V7X_VMEM_LIMIT = 56 * 1024 * 1024


def _params(*sem):
    return pltpu.CompilerParams(dimension_semantics=sem, vmem_limit_bytes=V7X_VMEM_LIMIT)


def _dot(a, b):
    return jnp.dot(a.astype(BF16), b.astype(BF16), preferred_element_type=F32)


def _dot_nt(a, b):
    return lax.dot_general(a.astype(BF16), b.astype(BF16), (((1,), (1,)), ((), ())),
                           preferred_element_type=F32)


def _dot_tn(a, b):
    return lax.dot_general(a.astype(BF16), b.astype(BF16), (((0,), (0,)), ((), ())),
                           preferred_element_type=F32)


def _split2(x):
    hi = x.astype(BF16)
    lo = (x - hi.astype(F32)).astype(BF16)
    return hi, lo


def _split3(x):
    hi = x.astype(BF16)
    r = x - hi.astype(F32)
    mid = r.astype(BF16)
    lo = (r - mid.astype(F32)).astype(BF16)
    return hi, mid, lo


def _lane_head(shape):
    return lax.broadcasted_iota(jnp.int32, shape, 1) // HEAD


def _bd(xw):
    xb = xw.astype(BF16)
    lh = _lane_head(xb.shape)
    zero = jnp.zeros_like(xb)
    return jnp.concatenate([jnp.where(lh == h, xb, zero) for h in range(NHEAD)], axis=0)


def _compact(full):
    lh = _lane_head((HEAD, GROUP))
    out = jnp.zeros((HEAD, GROUP), F32)
    for h in range(NHEAD):
        out = out + jnp.where(lh == h, full[h * HEAD:(h + 1) * HEAD, :], 0.0)
    return out


def _head_sum(x, ones_bd):
    hi, lo = _split2(x)
    return (jnp.dot(hi, ones_bd, preferred_element_type=F32)
            + jnp.dot(lo, ones_bd, preferred_element_type=F32))


def _ones_bd():
    r = lax.broadcasted_iota(jnp.int32, (GROUP, GROUP), 0) // HEAD
    c = lax.broadcasted_iota(jnp.int32, (GROUP, GROUP), 1) // HEAD
    return jnp.where(r == c, 1.0, 0.0).astype(BF16)


def _rwkv_prep_kernel(u_ref, up_ref, un_ref, mup_ref, mun_ref, w0_ref, w2_ref, a0_ref, a2_ref,
                      g2_ref, kk_ref, ka_ref, rk_ref,
                      logw_ref, kkn_ref, b_ref, kd_ref, v_ref, r_ref, bonus_ref, gate_ref, *, tm):
    i = pl.program_id(1)
    nt = pl.num_programs(1)
    u = u_ref[0]
    row8 = lax.broadcasted_iota(jnp.int32, (8, u.shape[1]), 0)
    prev_row = jnp.where(i == 0, 0.0, up_ref[0, 7:8, :])
    next_row = jnp.where(i == nt - 1, 0.0, un_ref[0, 0:1, :])
    prev = pltpu.roll(u, 1, 0)
    nxt = pltpu.roll(u, tm - 1, 0)
    prev = jnp.concatenate([jnp.where(row8 == 0, prev_row, prev[:8]), prev[8:]], axis=0)
    nxt = jnp.concatenate([nxt[:tm - 8], jnp.where(row8 == 7, next_row, nxt[tm - 8:])], axis=0)
    s = u + mup_ref[...] * (prev - u) + mun_ref[...] * (nxt - u)

    G = GROUP
    gl = s[:, 0:GATE_LORA]
    r = s[:, GATE_LORA:GATE_LORA + G]
    k = s[:, GATE_LORA + G:GATE_LORA + 2 * G]
    v = s[:, GATE_LORA + 2 * G:GATE_LORA + 3 * G]
    wl = s[:, GATE_LORA + 3 * G:GATE_LORA + 3 * G + 2 * LORA]
    al = s[:, GATE_LORA + 3 * G + 2 * LORA:]

    ones_bd = _ones_bd()
    wlin = _dot(jnp.tanh(wl), w2_ref[...]) + w0_ref[...]
    logw = -math.exp(-0.5) * jax.nn.sigmoid(wlin)
    a = jax.nn.sigmoid(_dot(al, a2_ref[...]) + a0_ref[...])

    kraw = k * kk_ref[...]
    kkn = kraw * lax.rsqrt(jnp.maximum(_head_sum(kraw * kraw, ones_bd), 1e-24))
    ka = ka_ref[...]
    kd_sum = jnp.zeros_like(k)
    for d in range(2):
        a_d = a[:, d * G:(d + 1) * G]
        kd = k * (1.0 + (a_d - 1.0) * ka)
        logw_ref[0, d] = logw[:, d * G:(d + 1) * G]
        b_ref[0, d] = kkn * a_d
        kd_ref[0, d] = kd
        kd_sum = kd_sum + kd
    coef = _head_sum(r * kd_sum * rk_ref[...], ones_bd)
    kkn_ref[0] = kkn
    v_ref[0] = v
    r_ref[0] = r
    bonus_ref[0] = coef * v
    gate_ref[0] = _dot(jax.nn.sigmoid(gl), g2_ref[...])


def _blockdiag2(w):
    z = jnp.zeros_like(w[0])
    return jnp.concatenate([jnp.concatenate([w[0], z], axis=1),
                            jnp.concatenate([z, w[1]], axis=1)], axis=0)


def _rwkv_prep(u, mu_prev, mu_next, w0, w2, a0, a2, g2, k_k, k_a, r_k, tm):
    B, T, C = u.shape
    assert C == RWKV_COLS and T % tm == 0 and tm % 8 == 0
    nt = T // tm
    hb = tm // 8
    row = lambda x: x.reshape(1, -1).astype(F32)
    tok = lambda: pl.BlockSpec((1, tm, GROUP), lambda b, i: (b, i, 0))
    tok2 = lambda: pl.BlockSpec((1, 2, tm, GROUP), lambda b, i: (b, 0, i, 0))
    full = lambda shp: pl.BlockSpec(shp, lambda b, i: (0,) * len(shp))
    sd = lambda: jax.ShapeDtypeStruct((B, T, GROUP), F32)
    sd2 = lambda: jax.ShapeDtypeStruct((B, 2, T, GROUP), F32)
    return pl.pallas_call(
        functools.partial(_rwkv_prep_kernel, tm=tm),
        grid=(B, nt),
        in_specs=[
            pl.BlockSpec((1, tm, C), lambda b, i: (b, i, 0)),
            pl.BlockSpec((1, 8, C), lambda b, i: (b, jnp.maximum(i * hb - 1, 0), 0)),
            pl.BlockSpec((1, 8, C), lambda b, i: (b, jnp.minimum((i + 1) * hb, T // 8 - 1), 0)),
            full((1, C)), full((1, C)),
            full((1, 2 * GROUP)), full((2 * LORA, 2 * GROUP)),
            full((1, 2 * GROUP)), full((2 * LORA, 2 * GROUP)),
            full((GATE_LORA, GROUP)), full((1, GROUP)), full((1, GROUP)), full((1, GROUP)),
        ],
        out_specs=[tok2(), tok(), tok2(), tok2(), tok(), tok(), tok(), tok()],
        out_shape=[sd2(), sd(), sd2(), sd2(), sd(), sd(), sd(), sd()],
        compiler_params=_params("parallel", "parallel"),
        name="rwkv_prep",
    )(u, u, u, row(mu_prev), row(mu_next), row(w0), _blockdiag2(w2).astype(BF16),
      row(a0), _blockdiag2(a2).astype(BF16), g2.astype(BF16), row(k_k), row(k_a), row(r_k))


def _wkv_chunk_kernel(logw_ref, kk_ref, b_ref, kd_ref, v_ref, r_ref,
                      g_ref, n_ref, pc_ref, rp_ref, y0_ref, *, nc, uf):
    d = pl.program_id(1)
    t_w = lax.broadcasted_iota(jnp.int32, (CHUNK, GROUP), 0)
    s_w = lax.broadcasted_iota(jnp.int32, (CHUNK, GROUP), 1) % HEAD
    sign = jnp.where(d == 0, 1, -1)
    ahead = (t_w - s_w) * sign
    strict = ahead > 0
    incl = ahead >= 0
    eye_w = jnp.where(s_w == t_w, 1.0, 0.0)
    diag8 = (t_w // 8) == (s_w // 8)
    offs = [((t_w // (2 * m)) == (s_w // (2 * m))) & ((t_w // m) != (s_w // m)) for m in (8, 16, 32)]
    t_s = lax.broadcasted_iota(jnp.int32, (CHUNK, CHUNK), 0)
    s_s = lax.broadcasted_iota(jnp.int32, (CHUNK, CHUNK), 1)
    tri = jnp.where((t_s - s_s) * sign >= 0, 1.0, 0.0).astype(BF16)

    def each(f, *lists):
        return [f(*args) for args in zip(*lists)]

    def body(cc, carry):
        cs_ = [cc * uf + q for q in range(uf)]
        rows = [pl.ds(pl.multiple_of(c * CHUNK, CHUNK), CHUNK) for c in cs_]
        logw = [logw_ref[0, 0, rw, :] for rw in rows]
        kk = [kk_ref[0, rw, :] for rw in rows]
        b = [b_ref[0, 0, rw, :] for rw in rows]
        kd = [kd_ref[0, 0, rw, :] for rw in rows]
        v = [v_ref[0, rw, :] for rw in rows]
        r = [r_ref[0, rw, :] for rw in rows]

        def cumsum(lw):
            l1, l2, l3 = _split3(lw)
            cs = jnp.dot(tri, jnp.concatenate([l1, l2, l3], axis=1), preferred_element_type=F32)
            return cs[:, :GROUP] + cs[:, GROUP:2 * GROUP] + cs[:, 2 * GROUP:]

        lcum = each(cumsum, logw)
        ltot = each(lambda lw: jnp.sum(lw, axis=0, keepdims=True), logw)
        e_incl = each(jnp.exp, lcum)
        e_inv = each(lambda z: jnp.exp(-z), lcum)
        e_prev = each(lambda z, lw: jnp.exp(z - lw), lcum, logw)
        e_end = each(lambda lt, z: jnp.exp(lt - z), ltot, lcum)
        a_t = each(lambda x_, e: -x_ * e, kk, e_prev)
        b_t = each(jnp.multiply, b, e_inv)
        k_t = each(jnp.multiply, kd, e_inv)
        r_t = each(jnp.multiply, r, e_incl)
        b_h = each(jnp.multiply, b, e_end)
        k_h = each(jnp.multiply, kd, e_end)

        ar = each(lambda x_, y_: jnp.concatenate([x_, y_], axis=0), a_t, r_t)
        p1 = each(lambda x_, y_: _dot_nt(x_, _bd(y_)), ar, b_t)
        p2 = each(lambda x_, y_: _dot_nt(x_, _bd(y_)), ar, k_t)
        a_ab = each(lambda p: jnp.where(strict, p[:CHUNK], 0.0), p1)
        a_rb = each(lambda p: jnp.where(incl, p[CHUNK:], 0.0), p1)
        a_ak = each(lambda p: jnp.where(strict, p[:CHUNK], 0.0), p2)
        a_rk = each(lambda p: jnp.where(incl, p[CHUNK:], 0.0), p2)

        mm = lambda x_, y_: _dot(x_, _bd(y_))
        l8 = each(lambda z: jnp.where(diag8, z, 0.0), a_ab)
        m = each(mm, l8, l8)
        tinv = each(lambda z: eye_w + z, l8)
        tm_ = each(lambda t_, m_: mm(jnp.concatenate([t_, m_], axis=0), m_), tinv, m)
        tinv = each(lambda t_, z: t_ + z[:CHUNK], tinv, tm_)
        tinv = each(lambda t_, z: t_ + mm(t_, z[CHUNK:]), tinv, tm_)
        for off in offs:
            lt_ = each(lambda z, t_: mm(jnp.where(off, z, 0.0), t_), a_ab, tinv)
            tinv = each(lambda t_, z: t_ + mm(t_, z), tinv, lt_)

        xv = each(lambda ak, rk, v_: mm(jnp.concatenate([ak, rk], axis=0), v_), a_ak, a_rk, v)
        a_p = each(mm, tinv, a_t)
        u0 = each(lambda t_, z: mm(t_, z[:CHUNK]), tinv, xv)
        r_p = each(lambda rt, arb, ap: rt + mm(arb, ap), r_t, a_rb, a_p)
        y0 = each(lambda arb, u_, z: mm(arb, u_) + z[CHUNK:], a_rb, u0, xv)
        g_c = each(lambda ap, bh: _compact(_dot_tn(ap, bh)), a_p, b_h)
        n_c = each(lambda u_, v_, bh, kh: _compact(_dot_tn(jnp.concatenate([u_, v_], axis=0),
                                                           jnp.concatenate([bh, kh], axis=0))),
                   u0, v, b_h, k_h)
        for q, c in enumerate(cs_):
            g_ref[0, 0, c] = g_c[q].astype(BF16)
            n_ref[0, 0, c] = n_c[q]
            pc_ref[0, 0, c] = jnp.exp(ltot[q])
            rp_ref[0, 0, rows[q], :] = r_p[q].astype(BF16)
            y0_ref[0, 0, rows[q], :] = y0[q]
        return carry

    lax.fori_loop(0, nc // uf, body, 0)


def _wkv_chunk(logw, kk, b, kd, v, r, tm):
    B, _, T, _ = logw.shape
    assert T % tm == 0 and tm % CHUNK == 0
    nt, nc = T // tm, tm // CHUNK
    nct = T // CHUNK
    tok = lambda: pl.BlockSpec((1, tm, GROUP), lambda bb, d, i: (bb, i, 0))
    tok2 = lambda: pl.BlockSpec((1, 1, tm, GROUP), lambda bb, d, i: (bb, d, i, 0))
    mat = lambda: pl.BlockSpec((1, 1, nc, HEAD, GROUP), lambda bb, d, i: (bb, d, i, 0, 0))
    return pl.pallas_call(
        functools.partial(_wkv_chunk_kernel, nc=nc, uf=math.gcd(nc, WKV_UNROLL)),
        grid=(B, 2, nt),
        in_specs=[tok2(), tok(), tok2(), tok2(), tok(), tok()],
        out_specs=[mat(), mat(),
                   pl.BlockSpec((1, 1, nc, 1, GROUP), lambda bb, d, i: (bb, d, i, 0, 0)),
                   tok2(), tok2()],
        out_shape=[jax.ShapeDtypeStruct((B, 2, nct, HEAD, GROUP), BF16),
                   jax.ShapeDtypeStruct((B, 2, nct, HEAD, GROUP), F32),
                   jax.ShapeDtypeStruct((B, 2, nct, 1, GROUP), F32),
                   jax.ShapeDtypeStruct((B, 2, T, GROUP), BF16),
                   jax.ShapeDtypeStruct((B, 2, T, GROUP), F32)],
        compiler_params=_params("parallel", "parallel", "parallel"),
        name="wkv_chunk",
    )(logw, kk, b, kd, v, r)


def _wkv_scan_kernel(gf_ref, gb_ref, nf_ref, nb_ref, pf_ref, pb_ref, s0_ref, sf_ref, sb_ref, send_ref, st_ref,
                     *, nblk, cb):
    i = pl.program_id(0)
    B = st_ref.shape[1]

    @pl.when(i == 0)
    def _():
        for d in range(2):
            st_ref[d] = s0_ref[:, d]

    def body(jj, carry):
        chains = [(d, bb, jj if d == 0 else cb - 1 - jj, g, n, p, o)
                  for d, (g, n, p, o) in enumerate(((gf_ref, nf_ref, pf_ref, sf_ref),
                                                    (gb_ref, nb_ref, pb_ref, sb_ref)))
                  for bb in range(B)]
        s = [st_ref[d, bb] for d, bb, *_ in chains]
        gbd = [_bd(g[bb, 0, j]) for _, bb, j, g, *_ in chains]
        prod = [_dot(s_, g_) for s_, g_ in zip(s, gbd)]
        for s_, pr, (d, bb, j, g, n, p, o) in zip(s, prod, chains):
            o[bb, j] = s_.astype(BF16)
            st_ref[d, bb] = s_ * p[bb, 0, j] + pr + n[bb, 0, j]
        return carry

    lax.fori_loop(0, cb, body, 0, unroll=2)

    @pl.when(i == nblk - 1)
    def _():
        for d in range(2):
            send_ref[:, d] = st_ref[d]


def _wkv_scan(g, n, pc, s0, cb):
    B, _, nct, _, _ = g.shape
    assert nct % cb == 0
    nblk = nct // cb
    fwd = lambda shp: pl.BlockSpec((B, 1) + shp, lambda i: (0, 0, i, 0, 0))
    bwd = lambda shp: pl.BlockSpec((B, 1) + shp, lambda i: (0, 1, nblk - 1 - i, 0, 0))
    mat, vec = (cb, HEAD, GROUP), (cb, 1, GROUP)
    st = pl.BlockSpec((B, 2, HEAD, GROUP), lambda i: (0, 0, 0, 0))
    return pl.pallas_call(
        functools.partial(_wkv_scan_kernel, nblk=nblk, cb=cb),
        grid=(nblk,),
        in_specs=[fwd(mat), bwd(mat), fwd(mat), bwd(mat), fwd(vec), bwd(vec), st],
        out_specs=[pl.BlockSpec((B,) + mat, lambda i: (0, i, 0, 0)),
                   pl.BlockSpec((B,) + mat, lambda i: (0, nblk - 1 - i, 0, 0)), st],
        out_shape=[jax.ShapeDtypeStruct((B, nct, HEAD, GROUP), BF16),
                   jax.ShapeDtypeStruct((B, nct, HEAD, GROUP), BF16),
                   jax.ShapeDtypeStruct((B, 2, HEAD, GROUP), F32)],
        scratch_shapes=[pltpu.VMEM((2, B, HEAD, GROUP), F32)],
        compiler_params=_params("arbitrary"),
        name="wkv_scan",
    )(g, g, n, n, pc, pc, s0)


def _wkv_out_kernel(rp_ref, y0_ref, sf_ref, sb_ref, bonus_ref, gate_ref, gnw_ref, gnb_ref, y_ref, o_ref,
                    *, nc, uf, rb):
    ones_bd = _ones_bd()

    def body(cc, carry):
        for q in range(uf):
            c = cc * uf + q
            rows = pl.ds(pl.multiple_of(c * CHUNK, CHUNK), CHUNK)
            y = y0_ref[0, 0, rows, :] + y0_ref[0, 1, rows, :]
            for d, s_ref in enumerate((sf_ref, sb_ref)):
                y = y + _dot_nt(rp_ref[0, d, rows, :], _bd(s_ref[0, c]))
            o_ref[0, rows, :] = y
        return carry

    lax.fori_loop(0, nc // uf, body, 0)

    for r0 in range(0, nc * CHUNK, rb):
        rows = slice(r0, r0 + rb)
        y = o_ref[0, rows, :]
        mu = _head_sum(y, ones_bd) * (1.0 / HEAD)
        yc = y - mu
        var = _head_sum(yc * yc, ones_bd) * (1.0 / HEAD)
        yn = yc * lax.rsqrt(var + GN_EPS) * gnw_ref[...] + gnb_ref[...]
        y_ref[0, rows, :] = ((yn + bonus_ref[0, rows, :]) * gate_ref[0, rows, :]).astype(BF16)


def _wkv_out(rp, y0, s_f, s_b, bonus, gate, gn_w, gn_b, tm):
    B, _, T, _ = rp.shape
    nt, nc = T // tm, tm // CHUNK
    tok = lambda: pl.BlockSpec((1, tm, GROUP), lambda bb, i: (bb, i, 0))
    tok2 = lambda: pl.BlockSpec((1, 2, tm, GROUP), lambda bb, i: (bb, 0, i, 0))
    vec = lambda: pl.BlockSpec((1, GROUP), lambda bb, i: (0, 0))
    return pl.pallas_call(
        functools.partial(_wkv_out_kernel, nc=nc, uf=math.gcd(nc, WKV_UNROLL), rb=min(tm, 256)),
        grid=(B, nt),
        in_specs=[tok2(), tok2(),
                  pl.BlockSpec((1, nc, HEAD, GROUP), lambda bb, i: (bb, i, 0, 0)),
                  pl.BlockSpec((1, nc, HEAD, GROUP), lambda bb, i: (bb, i, 0, 0)),
                  tok(), tok(), vec(), vec()],
        out_specs=tok(),
        out_shape=jax.ShapeDtypeStruct((B, T, GROUP), BF16),
        scratch_shapes=[pltpu.VMEM((1, tm, GROUP), F32)],
        compiler_params=_params("parallel", "parallel"),
        name="wkv_out",
    )(rp, y0, s_f, s_b, bonus, gate, gn_w.reshape(1, -1), gn_b.reshape(1, -1))


def _rwkv_stream(u, s0, lw, tm_prep, tm_chunk, cb):
    (mu_prev, mu_next, w0, w2, a0, a2, g2, k_k, k_a, r_k, gn_w, gn_b) = lw
    logw, kk, b, kd, v, r, bonus, gate = _rwkv_prep(u, mu_prev, mu_next, w0, w2, a0, a2, g2,
                                                    k_k, k_a, r_k, tm_prep)
    g, n, pc, rp, y0 = _wkv_chunk(logw, kk, b, kd, v, r, tm_chunk)
    s_f, s_b, send = _wkv_scan(g, n, pc, s0, cb)
    y = _wkv_out(rp, y0, s_f, s_b, bonus, gate, gn_w, gn_b, tm_chunk)
    return y, send


def _mod_kernel(c_ref, w_ref, b_ref, o_ref):
    cc = c_ref[...]
    o_ref[0] = _dot(cc * jax.nn.sigmoid(cc), w_ref[0]) + b_ref[0]


def _mod(cc, w_mod, b_mod, nb=4):
    L, D, N = w_mod.shape
    R = cc.shape[0]
    bn = N // nb
    return pl.pallas_call(
        _mod_kernel,
        grid=(L, nb),
        in_specs=[pl.BlockSpec((R, D), lambda l, j: (0, 0)),
                  pl.BlockSpec((1, D, bn), lambda l, j: (l, 0, j)),
                  pl.BlockSpec((1, 1, bn), lambda l, j: (l, 0, j))],
        out_specs=pl.BlockSpec((1, R, bn), lambda l, j: (l, 0, j)),
        out_shape=jax.ShapeDtypeStruct((L, R, N), F32),
        compiler_params=_params("parallel", "parallel"),
        name="adaln_mod",
    )(cc, w_mod, b_mod.reshape(L, 1, N))


def _modulated(h, g, shift, scale):
    y = h * lax.rsqrt(jnp.mean(h * h, axis=-1, keepdims=True) + RMS_EPS) * g
    return y * (1.0 + scale) + shift


def _proj_kernel(*refs, widths, has_pos):
    if has_pos:
        x_ref, rowtab_ref, coltab_ref, g_ref, sh_ref, sc_ref, w_ref, h_ref, *outs = refs
        half = coltab_ref.shape[1]
        for r in range(rowtab_ref.shape[0]):
            rows = slice(r * GRID_W, (r + 1) * GRID_W)
            h_ref[0, rows, :half] = x_ref[0, rows, :half] + rowtab_ref[r:r + 1, :]
            h_ref[0, rows, half:] = x_ref[0, rows, half:] + coltab_ref[...]
        h = h_ref[0]
    else:
        x_ref, g_ref, sh_ref, sc_ref, w_ref, *outs = refs
        h = x_ref[0]
    a = _modulated(h, g_ref[...], sh_ref[0], sc_ref[0]).astype(BF16)
    off = 0
    for o_ref, w in zip(outs, widths):
        o_ref[0] = jnp.dot(a, w_ref[:, off:off + w], preferred_element_type=F32)
        off += w


def _proj(h, g, shift, scale, w, widths, tm, pos=None):
    B, T, D = h.shape
    N = w.shape[1]
    assert sum(widths) == N and T % tm == 0
    tokd = pl.BlockSpec((1, tm, D), lambda b, i: (b, i, 0))
    vec = pl.BlockSpec((1, 1, D), lambda b, i: (b, 0, 0))
    in_specs = [tokd]
    args = [h]
    if pos is not None:
        rowtab, coltab = pos
        assert tm % GRID_W == 0 and (tm // GRID_W) % 8 == 0
        in_specs += [pl.BlockSpec((tm // GRID_W, D // 2), lambda b, i: (i, 0)),
                     pl.BlockSpec((GRID_W, D // 2), lambda b, i: (0, 0))]
        args += [rowtab, coltab]
    in_specs += [pl.BlockSpec((1, D), lambda b, i: (0, 0)), vec, vec,
                 pl.BlockSpec((D, N), lambda b, i: (0, 0))]
    args += [g.reshape(1, D), shift, scale, w]
    out_specs = [pl.BlockSpec((1, tm, wd), lambda b, i: (b, i, 0)) for wd in widths]
    out_shape = [jax.ShapeDtypeStruct((B, T, wd), F32) for wd in widths]
    if pos is not None:
        out_specs = [tokd] + out_specs
        out_shape = [jax.ShapeDtypeStruct((B, T, D), F32)] + out_shape
    return pl.pallas_call(
        functools.partial(_proj_kernel, widths=tuple(widths), has_pos=pos is not None),
        grid=(B, T // tm),
        in_specs=in_specs, out_specs=out_specs, out_shape=out_shape,
        compiler_params=_params("parallel", "parallel"),
        name="norm_proj",
    )(*args)


def _halo_specs(T, tm, C):
    hb = tm // HALO
    return [pl.BlockSpec((1, tm, C), lambda b, i: (b, i, 0)),
            pl.BlockSpec((1, HALO, C), lambda b, i: (b, jnp.maximum(i * hb - 1, 0), 0)),
            pl.BlockSpec((1, HALO, C), lambda b, i: (b, jnp.minimum((i + 1) * hb, T // HALO - 1), 0))]


def _pool_kernel(u_ref, up_ref, un_ref, hw_ref, w_ref, sc_ref, o_ref, x_ref, s2_ref, s4_ref, s8_ref, s16_ref,
                 *, tm, T):
    i = pl.program_id(1)
    P = POOL_HALO
    x_ref[0:P, :] = jnp.where(i == 0, 0.0, up_ref[0])
    x_ref[P:P + tm, :] = u_ref[0]
    x_ref[P + tm:, :] = jnp.where(i == pl.num_programs(1) - 1, 0.0, un_ref[0])
    n2, n4, n8, n16 = tm + 56, tm + 48, tm + 40, tm + 32
    s2_ref[...] = x_ref[0:n2, :] + x_ref[1:n2 + 1, :]
    s4_ref[...] = s2_ref[0:n4, :] + s2_ref[2:n4 + 2, :]
    s8_ref[...] = s4_ref[0:n8, :] + s4_ref[4:n8 + 4, :]
    s16_ref[...] = s8_ref[0:n16, :] + s8_ref[8:n16 + 8, :]
    lane = lax.broadcasted_iota(jnp.int32, (tm, GROUP), 1) // (GROUP // len(POOL_WINDOWS))
    pair = lambda ref, hw: ref[P - hw:P - hw + tm, :] + ref[P - hw + 1:P - hw + 1 + tm, :]
    total = jnp.where(lane == 0, pair(s2_ref, 1),
                      jnp.where(lane == 1, pair(s4_ref, 2),
                                jnp.where(lane == 2, pair(s8_ref, 4), pair(s16_ref, 8))))
    hw = hw_ref[...]
    t = i * tm + lax.broadcasted_iota(jnp.int32, (tm, GROUP), 0)
    clip = lambda z: jnp.clip(z, 0, T)
    count = (clip(t + hw) - clip(t - hw)) + (clip(t + hw + 1) - clip(t - hw + 1))
    diff = total / count.astype(F32) - u_ref[0]
    o_ref[0] = (_dot(diff, w_ref[...]) * sc_ref[...]).astype(BF16)


def _bd4_np(blocks):
    n = blocks[0].shape[0]
    out = np.zeros((4 * n, 4 * n), np.float64)
    for h, blk in enumerate(blocks):
        out[h * n:(h + 1) * n, h * n:(h + 1) * n] = blk
    return out


def _bd4(blocks):
    n = blocks.shape[1]
    z = jnp.zeros((n, n), blocks.dtype)
    return jnp.concatenate(
        [jnp.concatenate([blocks[h] if g == h else z for g in range(4)], axis=1) for h in range(4)], axis=0)


def _pool(u, pool_w, pool_scale, tm):
    B, T, C = u.shape
    P = POOL_HALO
    assert C == GROUP and T % tm == 0 and tm % P == 0 and POOL_WINDOWS == (2, 4, 8, 16)
    hw = np.repeat(np.array([w // 2 for w in POOL_WINDOWS], np.int32), GROUP // len(POOL_WINDOWS))
    full = lambda shp: pl.BlockSpec(shp, lambda b, i: (0,) * len(shp))
    hb = tm // P
    return pl.pallas_call(
        functools.partial(_pool_kernel, tm=tm, T=T),
        grid=(B, T // tm),
        in_specs=[pl.BlockSpec((1, tm, C), lambda b, i: (b, i, 0)),
                  pl.BlockSpec((1, P, C), lambda b, i: (b, jnp.maximum(i * hb - 1, 0), 0)),
                  pl.BlockSpec((1, P, C), lambda b, i: (b, jnp.minimum((i + 1) * hb, T // P - 1), 0)),
                  full((1, C)), full((C, C)), full((1, C))],
        out_specs=pl.BlockSpec((1, tm, C), lambda b, i: (b, i, 0)),
        out_shape=jax.ShapeDtypeStruct((B, T, C), BF16),
        scratch_shapes=[pltpu.VMEM((tm + 2 * P, C), F32), pltpu.VMEM((tm + 56, C), F32),
                        pltpu.VMEM((tm + 48, C), F32), pltpu.VMEM((tm + 40, C), F32),
                        pltpu.VMEM((tm + 32, C), F32)],
        compiler_params=_params("parallel", "parallel"),
        name="pool_mixer",
    )(u, u, u, jnp.asarray(hw).reshape(1, C), _bd4(pool_w).astype(BF16), pool_scale.reshape(1, C))


def _conv_kernel(u_ref, up_ref, un_ref, dw_ref, db_ref, lg_ref, lb_ref, pw_ref, o_ref, buf_ref, *, tm, rb):
    glu = lambda z: z[:, :GROUP] * jax.nn.sigmoid(z[:, GROUP:])
    i = pl.program_id(1)
    n = tm + 2 * HALO
    buf_ref[0, 0:HALO, :] = jnp.where(i == 0, 0.0, glu(up_ref[0]))
    buf_ref[0, HALO:HALO + tm, :] = glu(u_ref[0])
    buf_ref[0, HALO + tm:n, :] = jnp.where(i == pl.num_programs(1) - 1, 0.0, glu(un_ref[0]))
    buf_ref[0, n:n + 8, :] = jnp.zeros((8, GROUP), F32)
    for q in range(1, 8):
        buf_ref[q, 0:n, :] = buf_ref[0, q:q + n, :]
    half = CONV_WIDTH // 2
    for r0 in range(0, tm, rb):
        acc = jnp.zeros((rb, GROUP), F32) + db_ref[...]
        for j in range(CONV_WIDTH):
            lo = HALO + r0 + j - half
            acc = acc + dw_ref[j:j + 1, :] * buf_ref[lo % 8, lo - lo % 8:lo - lo % 8 + rb, :]
        mu = jnp.mean(acc, axis=-1, keepdims=True)
        xc = acc - mu
        var = jnp.mean(xc * xc, axis=-1, keepdims=True)
        hn = xc * lax.rsqrt(var + LN_EPS) * lg_ref[...] + lb_ref[...]
        o_ref[0, r0:r0 + rb, :] = _dot(hn * jax.nn.sigmoid(hn), pw_ref[...]).astype(BF16)


def _conv(u, dw_w, dw_b, ln_g, ln_b, pw, tm):
    B, T, C2 = u.shape
    C = C2 // 2
    assert C == GROUP and T % tm == 0 and HALO >= CONV_WIDTH // 2
    full = lambda shp: pl.BlockSpec(shp, lambda b, i: (0,) * len(shp))
    return pl.pallas_call(
        functools.partial(_conv_kernel, tm=tm, rb=min(tm, 128)),
        grid=(B, T // tm),
        in_specs=_halo_specs(T, tm, C2) + [full((CONV_WIDTH, C)), full((1, C)), full((1, C)), full((1, C)),
                                           full((C, C))],
        out_specs=pl.BlockSpec((1, tm, C), lambda b, i: (b, i, 0)),
        out_shape=jax.ShapeDtypeStruct((B, T, C), BF16),
        scratch_shapes=[pltpu.VMEM((8, tm + 2 * HALO + 8, C), F32)],
        compiler_params=_params("parallel", "parallel"),
        name="conv_module",
    )(u, u, u, dw_w, dw_b.reshape(1, C), ln_g.reshape(1, C), ln_b.reshape(1, C), pw.astype(BF16))


def _np_split(m):
    m = jnp.asarray(np.asarray(m, np.float32))
    hi = m.astype(BF16)
    lo = (m - hi.astype(F32)).astype(BF16)
    return hi, lo


def _dot3(a, b_hi, b_lo):
    a_hi, a_lo = _split2(a)
    return (jnp.dot(a_hi, b_hi, preferred_element_type=F32) + jnp.dot(a_lo, b_hi, preferred_element_type=F32)
            + jnp.dot(a_hi, b_lo, preferred_element_type=F32))


def _dot3c(c_hi, c_lo, a):
    a_hi, a_lo = _split2(a)
    return (jnp.dot(c_hi, a_hi, preferred_element_type=F32) + jnp.dot(c_hi, a_lo, preferred_element_type=F32)
            + jnp.dot(c_lo, a_hi, preferred_element_type=F32))


def _cos_sin(n):
    k = np.arange(n, dtype=np.float64)
    ang = 2.0 * np.pi * np.outer(k, k) / n
    return np.cos(ang), np.sin(ang)


def _chan_dft_np():
    c, s = _cos_sin(HEAD)
    return np.concatenate([_bd4_np([c] * 4), _bd4_np([s] * 4)], axis=1)


def _fourier_s1_kernel(x_ref, twc_ref, tws_ref, csh_ref, csl_ref, m1h_ref, m1l_ref, ore_ref, oim_ref,
                       *scr, tb):
    w = _dot3(x_ref[0].reshape(HEAD * tb, GROUP), csh_ref[...], csl_ref[...])
    for n, ref in enumerate(scr):
        ref[...] = w[:, n * 128:(n + 1) * 128]
    for t in range(tb):
        zz = jnp.concatenate(
            [jnp.concatenate([scr[2 * p][pl.ds(t, HEAD, stride=tb), :], scr[2 * p + 1][pl.ds(t, HEAD, stride=tb), :]],
                             axis=1) for p in range(2)], axis=0)
        y = _dot3c(m1h_ref[...], m1l_ref[...], zz)
        yre, yim = y[:HEAD], y[HEAD:]
        ct = jnp.concatenate([twc_ref[t], twc_ref[t]], axis=1)
        st = jnp.concatenate([tws_ref[t], tws_ref[t]], axis=1)
        ore_ref[0, t] = yre * ct + yim * st
        oim_ref[0, t] = yim * ct - yre * st


def _fourier_s2_kernel(yre_ref, yim_ref, c2h_ref, c2l_ref, fw_ref, o_ref, *scr, fb):
    n2 = yre_ref.shape[1]
    ins, outs = scr[:4], scr[4:]
    for p, ref in enumerate((yre_ref, yim_ref)):
        v = ref[0].reshape(n2 * fb, GROUP)
        ins[2 * p][...] = v[:, :128]
        ins[2 * p + 1][...] = v[:, 128:]
    for f in range(fb):
        rows = pl.ds(f, n2, stride=fb)
        yy = jnp.concatenate([jnp.concatenate([ins[2 * p][rows, :], ins[2 * p + 1][rows, :]], axis=1)
                              for p in range(2)], axis=0)
        res = _dot(_dot3c(c2h_ref[...], c2l_ref[...], yy), fw_ref[...])
        outs[0][rows, :] = res[:, :128]
        outs[1][rows, :] = res[:, 128:]
    o_ref[0] = jnp.concatenate([outs[0][...], outs[1][...]], axis=1).reshape(n2, fb, GROUP)


def _fourier_small_kernel(x_ref, csh_ref, csl_ref, cth_ref, ctl_ref, fw_ref, o_ref):
    w = _dot3(x_ref[0], csh_ref[...], csl_ref[...])
    zz = jnp.concatenate([w[:, :GROUP], w[:, GROUP:]], axis=0)
    o_ref[0] = _dot(_dot3c(cth_ref[...], ctl_ref[...], zz), fw_ref[...])


def _fourier(u, fourier_w, tb=8, fb=8):
    B, T, C = u.shape
    assert C == GROUP
    scale = 1.0 / math.sqrt(T * HEAD)
    csh, csl = _np_split(_chan_dft_np())
    fw = fourier_w.astype(BF16)
    full = lambda shp, n: pl.BlockSpec(shp, lambda *_: (0,) * n)
    if T <= 512:
        ct, st = _cos_sin(T)
        cth, ctl = _np_split(np.concatenate([ct, -st], axis=1) * scale)
        return pl.pallas_call(
            _fourier_small_kernel,
            grid=(B,),
            in_specs=[pl.BlockSpec((1, T, C), lambda b: (b, 0, 0)),
                      full((C, 2 * C), 2), full((C, 2 * C), 2), full((T, 2 * T), 2), full((T, 2 * T), 2),
                      full((C, C), 2)],
            out_specs=pl.BlockSpec((1, T, C), lambda b: (b, 0, 0)),
            out_shape=jax.ShapeDtypeStruct((B, T, C), F32),
            compiler_params=_params("parallel"),
            name="fourier_small",
        )(u, csh, csl, cth, ctl, fw)

    n1 = HEAD
    n2 = T // n1
    assert n1 * n2 == T and n2 % tb == 0 and n1 % fb == 0
    c1, s1 = _cos_sin(n1)
    m1h, m1l = _np_split(np.block([[c1, -s1], [-s1, -c1]]))
    f1 = np.arange(n1, dtype=np.float64)[None, :, None]
    t2 = np.arange(n2, dtype=np.float64)[:, None, None]
    ang = 2.0 * np.pi * f1 * t2 / T * np.ones((1, 1, 128))
    twc = jnp.asarray(np.cos(ang).astype(np.float32))
    tws = jnp.asarray(np.sin(ang).astype(np.float32))
    c2, s2 = _cos_sin(n2)
    c2h, c2l = _np_split(np.concatenate([c2, s2], axis=1) * scale)

    yre, yim = pl.pallas_call(
        functools.partial(_fourier_s1_kernel, tb=tb),
        grid=(n2 // tb, B),
        in_specs=[pl.BlockSpec((1, n1, tb, C), lambda j, b: (b, 0, j, 0)),
                  pl.BlockSpec((tb, n1, 128), lambda j, b: (j, 0, 0)),
                  pl.BlockSpec((tb, n1, 128), lambda j, b: (j, 0, 0)),
                  full((C, 2 * C), 2), full((C, 2 * C), 2), full((2 * n1, 2 * n1), 2), full((2 * n1, 2 * n1), 2)],
        out_specs=[pl.BlockSpec((1, tb, n1, C), lambda j, b: (b, j, 0, 0))] * 2,
        out_shape=[jax.ShapeDtypeStruct((B, n2, n1, C), F32)] * 2,
        scratch_shapes=[pltpu.VMEM((n1 * tb, 128), F32)] * 4,
        compiler_params=_params("parallel", "parallel"),
        name="fourier_stage1",
    )(u.reshape(B, n1, n2, C), twc, tws, csh, csl, m1h, m1l)

    out = pl.pallas_call(
        functools.partial(_fourier_s2_kernel, fb=fb),
        grid=(B, n1 // fb),
        in_specs=[pl.BlockSpec((1, n2, fb, C), lambda b, j: (b, 0, j, 0)),
                  pl.BlockSpec((1, n2, fb, C), lambda b, j: (b, 0, j, 0)),
                  full((n2, 2 * n2), 2), full((n2, 2 * n2), 2), full((C, C), 2)],
        out_specs=pl.BlockSpec((1, n2, fb, C), lambda b, j: (b, 0, j, 0)),
        out_shape=jax.ShapeDtypeStruct((B, n2, n1, C), F32),
        scratch_shapes=[pltpu.VMEM((n2 * fb, 128), F32)] * 6,
        compiler_params=_params("parallel", "parallel"),
        name="fourier_stage2",
    )(yre, yim, c2h, c2l, fw)
    return out.reshape(B, T, C)


def _tail_kernel(*refs, F, fc, final):
    (h_ref, y0_ref, y1_ref, y2_ref, y3_ref, wo_ref, gt1_ref, g_ref, sh_ref, sc_ref, gt2_ref,
     wi_ref, wf_ref) = refs[:13]
    o_ref = refs[-1]
    mix = None
    for n, y_ref in enumerate((y0_ref, y1_ref, y2_ref, y3_ref)):
        part = jnp.dot(y_ref[0].astype(BF16), wo_ref[n * GROUP:(n + 1) * GROUP, :], preferred_element_type=F32)
        mix = part if mix is None else mix + part
    h = h_ref[0] + gt1_ref[0] * mix
    a = _modulated(h, g_ref[...], sh_ref[0], sc_ref[0]).astype(BF16)
    acc = jnp.zeros(h.shape, F32)
    for j in range(F // fc):
        gate = jnp.dot(a, wi_ref[:, j * fc:(j + 1) * fc], preferred_element_type=F32)
        up = jnp.dot(a, wi_ref[:, F + j * fc:F + (j + 1) * fc], preferred_element_type=F32)
        mid = (gate * jax.nn.sigmoid(gate) * up).astype(BF16)
        acc = acc + jnp.dot(mid, wf_ref[j * fc:(j + 1) * fc, :], preferred_element_type=F32)
    out = h + gt2_ref[0] * acc
    if final:
        out = out * lax.rsqrt(jnp.mean(out * out, axis=-1, keepdims=True) + RMS_EPS) * refs[13][...]
    o_ref[0] = out


def _tail(h, ys, w_o, gt1, g, shift, scale, gt2, w_in, w_out, tm, final_g=None, fc=256):
    B, T, D = h.shape
    F = w_out.shape[0]
    assert F % fc == 0 and T % tm == 0
    tokd = pl.BlockSpec((1, tm, D), lambda b, i: (b, i, 0))
    tokg = pl.BlockSpec((1, tm, GROUP), lambda b, i: (b, i, 0))
    vec = pl.BlockSpec((1, 1, D), lambda b, i: (b, 0, 0))
    row = pl.BlockSpec((1, D), lambda b, i: (0, 0))
    const = lambda w: pl.BlockSpec(w.shape, lambda b, i: (0, 0), pipeline_mode=pl.Buffered(1))
    in_specs = [tokd, tokg, tokg, tokg, tokg, const(w_o), vec, row, vec, vec, vec, const(w_in), const(w_out)]
    args = [h, *ys, w_o, gt1, g.reshape(1, D), shift, scale, gt2, w_in, w_out]
    if final_g is not None:
        in_specs.append(row)
        args.append(final_g.reshape(1, D))
    return pl.pallas_call(
        functools.partial(_tail_kernel, F=F, fc=fc, final=final_g is not None),
        grid=(B, T // tm),
        in_specs=in_specs, out_specs=tokd,
        out_shape=jax.ShapeDtypeStruct((B, T, D), F32),
        compiler_params=_params("parallel", "parallel"),
        name="layer_tail",
    )(*args)


def _pos_tables(rows, dim):
    quarter = dim // 4
    omega = 1.0 / (POS_BASE ** (jnp.arange(quarter, dtype=F32) / quarter))

    def enc(p):
        ang = p[:, None] * omega[None, :]
        return jnp.concatenate([jnp.sin(ang), jnp.cos(ang)], axis=-1)

    return enc(jnp.arange(rows, dtype=F32)), enc(jnp.arange(GRID_W, dtype=F32))


def _mixers(u_pool, u_four, u_conv, lw, tm):
    pool_w, pool_scale, fourier_w, dw_w, dw_b, ln_g, ln_b, pw = lw
    return (_pool(u_pool, pool_w, pool_scale, tm), _fourier(u_four, fourier_w),
            _conv(u_conv, dw_w, dw_b, ln_g, ln_b, pw, tm))


def kernel(x, c, ctx, c_ctx, w_mod, b_mod, norm1_g, norm2_g, w_in, w_out, rwkv_mu_prev, rwkv_mu_next, rwkv_w0, rwkv_w2, rwkv_a0, rwkv_a2, rwkv_g2, rwkv_k_k, rwkv_k_a, rwkv_r_k, rwkv_gn_w, rwkv_gn_b, pool_w, pool_scale, fourier_w, conv_dw_w, conv_dw_b, conv_ln_g, conv_ln_b, conv_pw, ffn_w_in, ffn_w_out, final_norm_g):
    B, T, D = x.shape
    Tc = ctx.shape[1]
    depth = w_mod.shape[0]
    widths = (RWKV_COLS, GROUP, GROUP, 2 * GROUP)
    tm, tmc = 512, Tc

    cc = jnp.concatenate([c, c_ctx[None], jnp.zeros((8 - B - 1, D), F32)], axis=0)
    mod = _mod(cc, w_mod, b_mod)
    pos = _pos_tables(T // GRID_W, D)
    s_zero = jnp.zeros((B, 2, HEAD, GROUP), F32)

    h, hc = x, ctx
    for l in range(depth):
        last = l == depth - 1
        mx = [mod[l, :B, n * D:(n + 1) * D][:, None, :] for n in range(6)]
        mc = [jnp.broadcast_to(mod[l, B, n * D:(n + 1) * D][None, None, :], (B, 1, D)) for n in range(6)]
        w_in_b = w_in[l].astype(BF16)
        w_out_b = w_out[l].astype(BF16)
        ffn_in_b = ffn_w_in[l].astype(BF16)
        ffn_out_b = ffn_w_out[l].astype(BF16)
        rw = (rwkv_mu_prev[l], rwkv_mu_next[l], rwkv_w0[l], rwkv_w2[l], rwkv_a0[l], rwkv_a2[l], rwkv_g2[l],
              rwkv_k_k[l], rwkv_k_a[l], rwkv_r_k[l].reshape(-1), rwkv_gn_w[l], rwkv_gn_b[l])
        ow = (pool_w[l], pool_scale[l], fourier_w[l], conv_dw_w[l], conv_dw_b[l], conv_ln_g[l], conv_ln_b[l],
              conv_pw[l])

        if l == 0:
            h, ux_r, ux_p, ux_f, ux_c = _proj(h, norm1_g[l], mx[0], mx[1], w_in_b, widths, tm, pos=pos)
        else:
            ux_r, ux_p, ux_f, ux_c = _proj(h, norm1_g[l], mx[0], mx[1], w_in_b, widths, tm)
        if last:
            (uc_r,) = _proj(hc, norm1_g[l], mc[0], mc[1], w_in_b[:, :RWKV_COLS], (RWKV_COLS,), tmc)
        else:
            uc_r, uc_p, uc_f, uc_c = _proj(hc, norm1_g[l], mc[0], mc[1], w_in_b, widths, tmc)

        yc_r, s_ctx = _rwkv_stream(uc_r, s_zero, rw, tmc, tmc, Tc // CHUNK)
        yx_r, _ = _rwkv_stream(ux_r, s_ctx, rw, tm, tm, 16)

        yx = (yx_r,) + _mixers(ux_p, ux_f, ux_c, ow, tm)
        h = _tail(h, yx, w_out_b, mx[2], norm2_g[l], mx[3], mx[4], mx[5], ffn_in_b, ffn_out_b, tm,
                  final_g=final_norm_g if last else None)
        if not last:
            yc = (yc_r,) + _mixers(uc_p, uc_f, uc_c, ow, tmc)
            hc = _tail(hc, yc, w_out_b, mc[2], norm2_g[l], mc[3], mc[4], mc[5], ffn_in_b, ffn_out_b, tmc)
    return h
```

```python
import functools
import math

import jax
import jax.numpy as jnp
import numpy as np
from jax import lax
from jax.experimental import pallas as pl
from jax.experimental.pallas import tpu as pltpu

F32 = jnp.float32
BF16 = jnp.bfloat16

GROUP = 256
HEAD = 64
NHEAD = GROUP // HEAD
GATE_LORA = 128
LORA = 64
RWKV_COLS = GATE_LORA + 3 * GROUP + 4 * LORA
GRID_W = 64
POOL_WINDOWS = (2, 4, 8, 16)
CONV_WIDTH = 31
RMS_EPS = 1e-6
GN_EPS = 64e-5
LN_EPS = 1e-5
POS_BASE = 10000.0

CHUNK = 64
WKV_UNROLL = 8
HALO = 16
POOL_HALO = 32
V7X_VMEM_LIMIT = 56 * 1024 * 1024


def _params(*sem):
    return pltpu.CompilerParams(dimension_semantics=sem, vmem_limit_bytes=V7X_VMEM_LIMIT)


def _dot(a, b):
    return jnp.dot(a.astype(BF16), b.astype(BF16), preferred_element_type=F32)


def _dot_nt(a, b):
    return lax.dot_general(a.astype(BF16), b.astype(BF16), (((1,), (1,)), ((), ())),
                           preferred_element_type=F32)


def _dot_tn(a, b):
    return lax.dot_general(a.astype(BF16), b.astype(BF16), (((0,), (0,)), ((), ())),
                           preferred_element_type=F32)


def _split2(x):
    hi = x.astype(BF16)
    lo = (x - hi.astype(F32)).astype(BF16)
    return hi, lo


def _split3(x):
    hi = x.astype(BF16)
    r = x - hi.astype(F32)
    mid = r.astype(BF16)
    lo = (r - mid.astype(F32)).astype(BF16)
    return hi, mid, lo


def _lane_head(shape):
    return lax.broadcasted_iota(jnp.int32, shape, 1) // HEAD


def _bd(xw):
    xb = xw.astype(BF16)
    lh = _lane_head(xb.shape)
    zero = jnp.zeros_like(xb)
    return jnp.concatenate([jnp.where(lh == h, xb, zero) for h in range(NHEAD)], axis=0)


def _compact(full):
    lh = _lane_head((HEAD, GROUP))
    out = jnp.zeros((HEAD, GROUP), F32)
    for h in range(NHEAD):
        out = out + jnp.where(lh == h, full[h * HEAD:(h + 1) * HEAD, :], 0.0)
    return out


def _head_sum(x, ones_bd):
    hi, lo = _split2(x)
    return (jnp.dot(hi, ones_bd, preferred_element_type=F32)
            + jnp.dot(lo, ones_bd, preferred_element_type=F32))


def _ones_bd():
    r = lax.broadcasted_iota(jnp.int32, (GROUP, GROUP), 0) // HEAD
    c = lax.broadcasted_iota(jnp.int32, (GROUP, GROUP), 1) // HEAD
    return jnp.where(r == c, 1.0, 0.0).astype(BF16)


def _rwkv_prep_kernel(u_ref, up_ref, un_ref, mup_ref, mun_ref, w0_ref, w2_ref, a0_ref, a2_ref,
                      g2_ref, kk_ref, ka_ref, rk_ref,
                      logw_ref, kkn_ref, b_ref, kd_ref, v_ref, r_ref, bonus_ref, gate_ref, *, tm):
    i = pl.program_id(1)
    nt = pl.num_programs(1)
    u = u_ref[0]
    row8 = lax.broadcasted_iota(jnp.int32, (8, u.shape[1]), 0)
    prev_row = jnp.where(i == 0, 0.0, up_ref[0, 7:8, :])
    next_row = jnp.where(i == nt - 1, 0.0, un_ref[0, 0:1, :])
    prev = pltpu.roll(u, 1, 0)
    nxt = pltpu.roll(u, tm - 1, 0)
    prev = jnp.concatenate([jnp.where(row8 == 0, prev_row, prev[:8]), prev[8:]], axis=0)
    nxt = jnp.concatenate([nxt[:tm - 8], jnp.where(row8 == 7, next_row, nxt[tm - 8:])], axis=0)
    mup, mun = mup_ref[...], mun_ref[...]
    s = u * (1.0 - mup - mun) + mup * prev + mun * nxt

    G = GROUP
    gl = s[:, 0:GATE_LORA]
    r = s[:, GATE_LORA:GATE_LORA + G]
    k = s[:, GATE_LORA + G:GATE_LORA + 2 * G]
    v = s[:, GATE_LORA + 2 * G:GATE_LORA + 3 * G]
    wl = s[:, GATE_LORA + 3 * G:GATE_LORA + 3 * G + 2 * LORA]
    al = s[:, GATE_LORA + 3 * G + 2 * LORA:]

    ones_bd = _ones_bd()
    wlin = _dot(jnp.tanh(wl), w2_ref[...]) + w0_ref[...]
    logw = -math.exp(-0.5) * jax.nn.sigmoid(wlin)
    a = jax.nn.sigmoid(_dot(al, a2_ref[...]) + a0_ref[...])

    kraw = k * kk_ref[...]
    kkn = kraw * lax.rsqrt(jnp.maximum(_head_sum(kraw * kraw, ones_bd), 1e-24))
    ka = ka_ref[...]
    kd_sum = jnp.zeros_like(k)
    for d in range(2):
        a_d = a[:, d * G:(d + 1) * G]
        kd = k * (1.0 + (a_d - 1.0) * ka)
        logw_ref[0, d] = logw[:, d * G:(d + 1) * G]
        b_ref[0, d] = kkn * a_d
        kd_ref[0, d] = kd
        kd_sum = kd_sum + kd
    coef = _head_sum(r * kd_sum * rk_ref[...], ones_bd)
    kkn_ref[0] = kkn
    v_ref[0] = v.astype(BF16)
    r_ref[0] = r
    bonus_ref[0] = coef * v
    gate_ref[0] = _dot(jax.nn.sigmoid(gl), g2_ref[...])


def _blockdiag2(w):
    z = jnp.zeros_like(w[0])
    return jnp.concatenate([jnp.concatenate([w[0], z], axis=1),
                            jnp.concatenate([z, w[1]], axis=1)], axis=0)


def _rwkv_prep(u, mu_prev, mu_next, w0, w2, a0, a2, g2, k_k, k_a, r_k, tm):
    B, T, C = u.shape
    assert C == RWKV_COLS and T % tm == 0 and tm % 8 == 0
    nt = T // tm
    hb = tm // 8
    row = lambda x: x.reshape(1, -1).astype(F32)
    tok = lambda: pl.BlockSpec((1, tm, GROUP), lambda b, i: (b, i, 0))
    tok2 = lambda: pl.BlockSpec((1, 2, tm, GROUP), lambda b, i: (b, 0, i, 0))
    full = lambda shp: pl.BlockSpec(shp, lambda b, i: (0,) * len(shp))
    sd = lambda: jax.ShapeDtypeStruct((B, T, GROUP), F32)
    sd2 = lambda: jax.ShapeDtypeStruct((B, 2, T, GROUP), F32)
    return pl.pallas_call(
        functools.partial(_rwkv_prep_kernel, tm=tm),
        grid=(B, nt),
        in_specs=[
            pl.BlockSpec((1, tm, C), lambda b, i: (b, i, 0)),
            pl.BlockSpec((1, 8, C), lambda b, i: (b, jnp.maximum(i * hb - 1, 0), 0)),
            pl.BlockSpec((1, 8, C), lambda b, i: (b, jnp.minimum((i + 1) * hb, T // 8 - 1), 0)),
            full((1, C)), full((1, C)),
            full((1, 2 * GROUP)), full((2 * LORA, 2 * GROUP)),
            full((1, 2 * GROUP)), full((2 * LORA, 2 * GROUP)),
            full((GATE_LORA, GROUP)), full((1, GROUP)), full((1, GROUP)), full((1, GROUP)),
        ],
        out_specs=[tok2(), tok(), tok2(), tok2(), tok(), tok(), tok(), tok()],
        out_shape=[sd2(), sd(), sd2(), sd2(), jax.ShapeDtypeStruct((B, T, GROUP), BF16), sd(), sd(), sd()],
        compiler_params=_params("parallel", "parallel"),
        name="rwkv_prep",
    )(u, u, u, row(mu_prev), row(mu_next), row(w0), _blockdiag2(w2).astype(BF16),
      row(a0), _blockdiag2(a2).astype(BF16), g2.astype(BF16), row(k_k), row(k_a), row(r_k))


def _wkv_chunk_kernel(logw_ref, kk_ref, b_ref, kd_ref, v_ref, r_ref,
                      g_ref, n_ref, pc_ref, rp_ref, y0_ref, *, nc, uf):
    d = pl.program_id(1)
    t_w = lax.broadcasted_iota(jnp.int32, (CHUNK, GROUP), 0)
    s_w = lax.broadcasted_iota(jnp.int32, (CHUNK, GROUP), 1) % HEAD
    sign = jnp.where(d == 0, 1, -1)
    ahead = (t_w - s_w) * sign
    strict = ahead > 0
    incl = ahead >= 0
    eye_w = jnp.where(s_w == t_w, 1.0, 0.0)
    diag8 = (t_w // 8) == (s_w // 8)
    offs = [((t_w // (2 * m)) == (s_w // (2 * m))) & ((t_w // m) != (s_w // m)) for m in (8, 16, 32)]
    t_s = lax.broadcasted_iota(jnp.int32, (CHUNK, CHUNK), 0)
    s_s = lax.broadcasted_iota(jnp.int32, (CHUNK, CHUNK), 1)
    tri = jnp.where((t_s - s_s) * sign >= 0, 1.0, 0.0).astype(BF16)

    def each(f, *lists):
        return [f(*args) for args in zip(*lists)]

    def body(cc, carry):
        cs_ = [cc * uf + q for q in range(uf)]
        rows = [pl.ds(pl.multiple_of(c * CHUNK, CHUNK), CHUNK) for c in cs_]
        logw = [logw_ref[0, 0, rw, :] for rw in rows]
        kk = [kk_ref[0, rw, :] for rw in rows]
        b = [b_ref[0, 0, rw, :] for rw in rows]
        kd = [kd_ref[0, 0, rw, :] for rw in rows]
        v = [v_ref[0, rw, :] for rw in rows]
        r = [r_ref[0, rw, :] for rw in rows]

        def cumsum(lw):
            l1, l2, l3 = _split3(lw)
            cs = jnp.dot(tri, jnp.concatenate([l1, l2, l3], axis=1), preferred_element_type=F32)
            return cs[:, :GROUP] + cs[:, GROUP:2 * GROUP] + cs[:, 2 * GROUP:]

        lcum = each(cumsum, logw)
        ltot = each(lambda lw: jnp.sum(lw, axis=0, keepdims=True), logw)
        e_incl = each(jnp.exp, lcum)
        e_inv = each(lambda z: jnp.exp(-z), lcum)
        e_prev = each(lambda z, lw: jnp.exp(z - lw), lcum, logw)
        e_end = each(lambda lt, z: jnp.exp(lt - z), ltot, lcum)
        a_t = each(lambda x_, e: -x_ * e, kk, e_prev)
        b_t = each(jnp.multiply, b, e_inv)
        k_t = each(jnp.multiply, kd, e_inv)
        r_t = each(jnp.multiply, r, e_incl)
        b_h = each(jnp.multiply, b, e_end)
        k_h = each(jnp.multiply, kd, e_end)

        ar = each(lambda x_, y_: jnp.concatenate([x_, y_], axis=0), a_t, r_t)
        p1 = each(lambda x_, y_: _dot_nt(x_, _bd(y_)), ar, b_t)
        p2 = each(lambda x_, y_: _dot_nt(x_, _bd(y_)), ar, k_t)
        a_ab = each(lambda p: jnp.where(strict, p[:CHUNK], 0.0), p1)
        a_rb = each(lambda p: jnp.where(incl, p[CHUNK:], 0.0), p1)
        a_ak = each(lambda p: jnp.where(strict, p[:CHUNK], 0.0), p2)
        a_rk = each(lambda p: jnp.where(incl, p[CHUNK:], 0.0), p2)

        mm = lambda x_, y_: _dot(x_, _bd(y_))
        l8 = each(lambda z: jnp.where(diag8, z, 0.0), a_ab)
        m = each(mm, l8, l8)
        tinv = each(lambda z: eye_w + z, l8)
        tm_ = each(lambda t_, m_: mm(jnp.concatenate([t_, m_], axis=0), m_), tinv, m)
        tinv = each(lambda t_, z: t_ + z[:CHUNK], tinv, tm_)
        tinv = each(lambda t_, z: t_ + mm(t_, z[CHUNK:]), tinv, tm_)
        for off in offs:
            lt_ = each(lambda z, t_: mm(jnp.where(off, z, 0.0), t_), a_ab, tinv)
            tinv = each(lambda t_, z: t_ + mm(t_, z), tinv, lt_)

        xv = each(lambda ak, rk, v_: mm(jnp.concatenate([ak, rk], axis=0), v_), a_ak, a_rk, v)
        a_p = each(mm, tinv, a_t)
        u0 = each(lambda t_, z: mm(t_, z[:CHUNK]), tinv, xv)
        r_p = each(lambda rt, arb, ap: rt + mm(arb, ap), r_t, a_rb, a_p)
        y0 = each(lambda arb, u_, z: mm(arb, u_) + z[CHUNK:], a_rb, u0, xv)
        g_c = each(lambda ap, bh: _compact(_dot_tn(ap, bh)), a_p, b_h)
        n_c = each(lambda u_, v_, bh, kh: _compact(_dot_tn(jnp.concatenate([u_, v_], axis=0),
                                                           jnp.concatenate([bh, kh], axis=0))),
                   u0, v, b_h, k_h)
        for q, c in enumerate(cs_):
            g_ref[0, 0, c] = g_c[q].astype(BF16)
            n_ref[0, 0, c] = n_c[q]
            pc_ref[0, 0, c] = jnp.exp(ltot[q])
            rp_ref[0, 0, rows[q], :] = r_p[q].astype(BF16)
            y0_ref[0, 0, rows[q], :] = y0[q]
        return carry

    lax.fori_loop(0, nc // uf, body, 0)


def _wkv_chunk(logw, kk, b, kd, v, r, tm):
    B, _, T, _ = logw.shape
    assert T % tm == 0 and tm % CHUNK == 0
    nt, nc = T // tm, tm // CHUNK
    nct = T // CHUNK
    tok = lambda: pl.BlockSpec((1, tm, GROUP), lambda bb, d, i: (bb, i, 0))
    tok2 = lambda: pl.BlockSpec((1, 1, tm, GROUP), lambda bb, d, i: (bb, d, i, 0))
    mat = lambda: pl.BlockSpec((1, 1, nc, HEAD, GROUP), lambda bb, d, i: (bb, d, i, 0, 0))
    return pl.pallas_call(
        functools.partial(_wkv_chunk_kernel, nc=nc, uf=math.gcd(nc, WKV_UNROLL)),
        grid=(B, 2, nt),
        in_specs=[tok2(), tok(), tok2(), tok2(), tok(), tok()],
        out_specs=[mat(), mat(),
                   pl.BlockSpec((1, 1, nc, 1, GROUP), lambda bb, d, i: (bb, d, i, 0, 0)),
                   tok2(), tok2()],
        out_shape=[jax.ShapeDtypeStruct((B, 2, nct, HEAD, GROUP), BF16),
                   jax.ShapeDtypeStruct((B, 2, nct, HEAD, GROUP), F32),
                   jax.ShapeDtypeStruct((B, 2, nct, 1, GROUP), F32),
                   jax.ShapeDtypeStruct((B, 2, T, GROUP), BF16),
                   jax.ShapeDtypeStruct((B, 2, T, GROUP), F32)],
        compiler_params=_params("parallel", "parallel", "parallel"),
        name="wkv_chunk",
    )(logw, kk, b, kd, v, r)


def _wkv_scan_kernel(gf_ref, gb_ref, nf_ref, nb_ref, pf_ref, pb_ref, s0_ref, sf_ref, sb_ref, send_ref, st_ref,
                     *, nblk, cb):
    i = pl.program_id(0)
    B = st_ref.shape[1]

    @pl.when(i == 0)
    def _():
        for d in range(2):
            st_ref[d] = s0_ref[:, d]

    def body(jj, carry):
        chains = [(d, bb, jj if d == 0 else cb - 1 - jj, g, n, p, o)
                  for d, (g, n, p, o) in enumerate(((gf_ref, nf_ref, pf_ref, sf_ref),
                                                    (gb_ref, nb_ref, pb_ref, sb_ref)))
                  for bb in range(B)]
        s = [st_ref[d, bb] for d, bb, *_ in chains]
        gbd = [_bd(g[bb, 0, j]) for _, bb, j, g, *_ in chains]
        prod = [_dot(s_, g_) for s_, g_ in zip(s, gbd)]
        for s_, pr, (d, bb, j, g, n, p, o) in zip(s, prod, chains):
            o[bb, j] = s_.astype(BF16)
            st_ref[d, bb] = s_ * p[bb, 0, j] + pr + n[bb, 0, j]
        return carry

    lax.fori_loop(0, cb, body, 0, unroll=2)

    @pl.when(i == nblk - 1)
    def _():
        for d in range(2):
            send_ref[:, d] = st_ref[d]


def _wkv_scan(g, n, pc, s0, cb):
    B, _, nct, _, _ = g.shape
    assert nct % cb == 0
    nblk = nct // cb
    fwd = lambda shp: pl.BlockSpec((B, 1) + shp, lambda i: (0, 0, i, 0, 0))
    bwd = lambda shp: pl.BlockSpec((B, 1) + shp, lambda i: (0, 1, nblk - 1 - i, 0, 0))
    mat, vec = (cb, HEAD, GROUP), (cb, 1, GROUP)
    st = pl.BlockSpec((B, 2, HEAD, GROUP), lambda i: (0, 0, 0, 0))
    return pl.pallas_call(
        functools.partial(_wkv_scan_kernel, nblk=nblk, cb=cb),
        grid=(nblk,),
        in_specs=[fwd(mat), bwd(mat), fwd(mat), bwd(mat), fwd(vec), bwd(vec), st],
        out_specs=[pl.BlockSpec((B,) + mat, lambda i: (0, i, 0, 0)),
                   pl.BlockSpec((B,) + mat, lambda i: (0, nblk - 1 - i, 0, 0)), st],
        out_shape=[jax.ShapeDtypeStruct((B, nct, HEAD, GROUP), BF16),
                   jax.ShapeDtypeStruct((B, nct, HEAD, GROUP), BF16),
                   jax.ShapeDtypeStruct((B, 2, HEAD, GROUP), F32)],
        scratch_shapes=[pltpu.VMEM((2, B, HEAD, GROUP), F32)],
        compiler_params=_params("arbitrary"),
        name="wkv_scan",
    )(g, g, n, n, pc, pc, s0)


def _wkv_out_kernel(rp_ref, y0_ref, sf_ref, sb_ref, bonus_ref, gate_ref, gnw_ref, gnb_ref, y_ref, o_ref,
                    *, nc, uf, rb):
    ones_bd = _ones_bd()

    def body(cc, carry):
        for q in range(uf):
            c = cc * uf + q
            rows = pl.ds(pl.multiple_of(c * CHUNK, CHUNK), CHUNK)
            y = y0_ref[0, 0, rows, :] + y0_ref[0, 1, rows, :]
            for d, s_ref in enumerate((sf_ref, sb_ref)):
                y = y + _dot_nt(rp_ref[0, d, rows, :], _bd(s_ref[0, c]))
            o_ref[0, rows, :] = y
        return carry

    lax.fori_loop(0, nc // uf, body, 0)

    for r0 in range(0, nc * CHUNK, rb):
        rows = slice(r0, r0 + rb)
        y = o_ref[0, rows, :]
        mu = _head_sum(y, ones_bd) * (1.0 / HEAD)
        yc = y - mu
        var = _head_sum(yc * yc, ones_bd) * (1.0 / HEAD)
        yn = yc * lax.rsqrt(var + GN_EPS) * gnw_ref[...] + gnb_ref[...]
        y_ref[0, rows, :] = ((yn + bonus_ref[0, rows, :]) * gate_ref[0, rows, :]).astype(BF16)


def _wkv_out(rp, y0, s_f, s_b, bonus, gate, gn_w, gn_b, tm):
    B, _, T, _ = rp.shape
    nt, nc = T // tm, tm // CHUNK
    tok = lambda: pl.BlockSpec((1, tm, GROUP), lambda bb, i: (bb, i, 0))
    tok2 = lambda: pl.BlockSpec((1, 2, tm, GROUP), lambda bb, i: (bb, 0, i, 0))
    vec = lambda: pl.BlockSpec((1, GROUP), lambda bb, i: (0, 0))
    return pl.pallas_call(
        functools.partial(_wkv_out_kernel, nc=nc, uf=math.gcd(nc, WKV_UNROLL), rb=min(tm, 256)),
        grid=(B, nt),
        in_specs=[tok2(), tok2(),
                  pl.BlockSpec((1, nc, HEAD, GROUP), lambda bb, i: (bb, i, 0, 0)),
                  pl.BlockSpec((1, nc, HEAD, GROUP), lambda bb, i: (bb, i, 0, 0)),
                  tok(), tok(), vec(), vec()],
        out_specs=tok(),
        out_shape=jax.ShapeDtypeStruct((B, T, GROUP), BF16),
        scratch_shapes=[pltpu.VMEM((1, tm, GROUP), F32)],
        compiler_params=_params("parallel", "parallel"),
        name="wkv_out",
    )(rp, y0, s_f, s_b, bonus, gate, gn_w.reshape(1, -1), gn_b.reshape(1, -1))


def _rwkv_stream(u, s0, lw, tm_prep, tm_chunk, cb):
    (mu_prev, mu_next, w0, w2, a0, a2, g2, k_k, k_a, r_k, gn_w, gn_b) = lw
    logw, kk, b, kd, v, r, bonus, gate = _rwkv_prep(u, mu_prev, mu_next, w0, w2, a0, a2, g2,
                                                    k_k, k_a, r_k, tm_prep)
    g, n, pc, rp, y0 = _wkv_chunk(logw, kk, b, kd, v, r, tm_chunk)
    s_f, s_b, send = _wkv_scan(g, n, pc, s0, cb)
    y = _wkv_out(rp, y0, s_f, s_b, bonus, gate, gn_w, gn_b, tm_chunk)
    return y, send


def _mod_kernel(c_ref, w_ref, b_ref, o_ref):
    cc = c_ref[...]
    o_ref[0] = _dot(cc * jax.nn.sigmoid(cc), w_ref[0]) + b_ref[0]


def _mod(cc, w_mod, b_mod, nb=4):
    L, D, N = w_mod.shape
    R = cc.shape[0]
    bn = N // nb
    return pl.pallas_call(
        _mod_kernel,
        grid=(L, nb),
        in_specs=[pl.BlockSpec((R, D), lambda l, j: (0, 0)),
                  pl.BlockSpec((1, D, bn), lambda l, j: (l, 0, j)),
                  pl.BlockSpec((1, 1, bn), lambda l, j: (l, 0, j))],
        out_specs=pl.BlockSpec((1, R, bn), lambda l, j: (l, 0, j)),
        out_shape=jax.ShapeDtypeStruct((L, R, N), F32),
        compiler_params=_params("parallel", "parallel"),
        name="adaln_mod",
    )(cc, w_mod, b_mod.reshape(L, 1, N))


def _modulated(h, g, shift, scale):
    y = h * lax.rsqrt(jnp.mean(h * h, axis=-1, keepdims=True) + RMS_EPS) * g
    return y * (1.0 + scale) + shift


def _proj_kernel(*refs, widths, has_pos):
    if has_pos:
        x_ref, rowtab_ref, coltab_ref, g_ref, sh_ref, sc_ref, w_ref, h_ref, *outs = refs
        half = coltab_ref.shape[1]
        for r in range(rowtab_ref.shape[0]):
            rows = slice(r * GRID_W, (r + 1) * GRID_W)
            h_ref[0, rows, :half] = x_ref[0, rows, :half] + rowtab_ref[r:r + 1, :]
            h_ref[0, rows, half:] = x_ref[0, rows, half:] + coltab_ref[...]
        h = h_ref[0]
    else:
        x_ref, g_ref, sh_ref, sc_ref, w_ref, *outs = refs
        h = x_ref[0]
    a = _modulated(h, g_ref[...], sh_ref[0], sc_ref[0]).astype(BF16)
    off = 0
    for o_ref, w in zip(outs, widths):
        o_ref[0] = jnp.dot(a, w_ref[:, off:off + w], preferred_element_type=F32)
        off += w


def _proj(h, g, shift, scale, w, widths, tm, pos=None):
    B, T, D = h.shape
    N = w.shape[1]
    assert sum(widths) == N and T % tm == 0
    tokd = pl.BlockSpec((1, tm, D), lambda b, i: (b, i, 0))
    vec = pl.BlockSpec((1, 1, D), lambda b, i: (b, 0, 0))
    in_specs = [tokd]
    args = [h]
    if pos is not None:
        rowtab, coltab = pos
        assert tm % GRID_W == 0 and (tm // GRID_W) % 8 == 0
        in_specs += [pl.BlockSpec((tm // GRID_W, D // 2), lambda b, i: (i, 0)),
                     pl.BlockSpec((GRID_W, D // 2), lambda b, i: (0, 0))]
        args += [rowtab, coltab]
    in_specs += [pl.BlockSpec((1, D), lambda b, i: (0, 0)), vec, vec,
                 pl.BlockSpec((D, N), lambda b, i: (0, 0))]
    args += [g.reshape(1, D), shift, scale, w]
    out_specs = [pl.BlockSpec((1, tm, wd), lambda b, i: (b, i, 0)) for wd in widths]
    out_shape = [jax.ShapeDtypeStruct((B, T, wd), F32) for wd in widths]
    if pos is not None:
        out_specs = [tokd] + out_specs
        out_shape = [jax.ShapeDtypeStruct((B, T, D), F32)] + out_shape
    return pl.pallas_call(
        functools.partial(_proj_kernel, widths=tuple(widths), has_pos=pos is not None),
        grid=(B, T // tm),
        in_specs=in_specs, out_specs=out_specs, out_shape=out_shape,
        compiler_params=_params("parallel", "parallel"),
        name="norm_proj",
    )(*args)


def _halo_specs(T, tm, C):
    hb = tm // HALO
    return [pl.BlockSpec((1, tm, C), lambda b, i: (b, i, 0)),
            pl.BlockSpec((1, HALO, C), lambda b, i: (b, jnp.maximum(i * hb - 1, 0), 0)),
            pl.BlockSpec((1, HALO, C), lambda b, i: (b, jnp.minimum((i + 1) * hb, T // HALO - 1), 0))]


def _pool_kernel(u_ref, up_ref, un_ref, hw_ref, w_ref, sc_ref, o_ref, x_ref, s2_ref, s4_ref, s8_ref, s16_ref,
                 *, tm, T):
    i = pl.program_id(1)
    P = POOL_HALO
    x_ref[0:P, :] = jnp.where(i == 0, 0.0, up_ref[0])
    x_ref[P:P + tm, :] = u_ref[0]
    x_ref[P + tm:, :] = jnp.where(i == pl.num_programs(1) - 1, 0.0, un_ref[0])
    n2, n4, n8, n16 = tm + 56, tm + 48, tm + 40, tm + 32
    s2_ref[...] = x_ref[0:n2, :] + x_ref[1:n2 + 1, :]
    s4_ref[...] = s2_ref[0:n4, :] + s2_ref[2:n4 + 2, :]
    s8_ref[...] = s4_ref[0:n8, :] + s4_ref[4:n8 + 4, :]
    s16_ref[...] = s8_ref[0:n16, :] + s8_ref[8:n16 + 8, :]
    lane = lax.broadcasted_iota(jnp.int32, (tm, GROUP), 1) // (GROUP // len(POOL_WINDOWS))
    pair = lambda ref, hw: ref[P - hw:P - hw + tm, :] + ref[P - hw + 1:P - hw + 1 + tm, :]
    total = jnp.where(lane == 0, pair(s2_ref, 1),
                      jnp.where(lane == 1, pair(s4_ref, 2),
                                jnp.where(lane == 2, pair(s8_ref, 4), pair(s16_ref, 8))))
    hw = hw_ref[...]
    t = i * tm + lax.broadcasted_iota(jnp.int32, (tm, GROUP), 0)
    clip = lambda z: jnp.clip(z, 0, T)
    count = (clip(t + hw) - clip(t - hw)) + (clip(t + hw + 1) - clip(t - hw + 1))
    diff = total / count.astype(F32) - u_ref[0]
    o_ref[0] = (_dot(diff, w_ref[...]) * sc_ref[...]).astype(BF16)


def _bd4_np(blocks):
    n = blocks[0].shape[0]
    out = np.zeros((4 * n, 4 * n), np.float64)
    for h, blk in enumerate(blocks):
        out[h * n:(h + 1) * n, h * n:(h + 1) * n] = blk
    return out


def _bd4(blocks):
    n = blocks.shape[1]
    z = jnp.zeros((n, n), blocks.dtype)
    return jnp.concatenate(
        [jnp.concatenate([blocks[h] if g == h else z for g in range(4)], axis=1) for h in range(4)], axis=0)


def _pool(u, pool_w, pool_scale, tm):
    B, T, C = u.shape
    P = POOL_HALO
    assert C == GROUP and T % tm == 0 and tm % P == 0 and POOL_WINDOWS == (2, 4, 8, 16)
    hw = np.repeat(np.array([w // 2 for w in POOL_WINDOWS], np.int32), GROUP // len(POOL_WINDOWS))
    full = lambda shp: pl.BlockSpec(shp, lambda b, i: (0,) * len(shp))
    hb = tm // P
    return pl.pallas_call(
        functools.partial(_pool_kernel, tm=tm, T=T),
        grid=(B, T // tm),
        in_specs=[pl.BlockSpec((1, tm, C), lambda b, i: (b, i, 0)),
                  pl.BlockSpec((1, P, C), lambda b, i: (b, jnp.maximum(i * hb - 1, 0), 0)),
                  pl.BlockSpec((1, P, C), lambda b, i: (b, jnp.minimum((i + 1) * hb, T // P - 1), 0)),
                  full((1, C)), full((C, C)), full((1, C))],
        out_specs=pl.BlockSpec((1, tm, C), lambda b, i: (b, i, 0)),
        out_shape=jax.ShapeDtypeStruct((B, T, C), BF16),
        scratch_shapes=[pltpu.VMEM((tm + 2 * P, C), F32), pltpu.VMEM((tm + 56, C), F32),
                        pltpu.VMEM((tm + 48, C), F32), pltpu.VMEM((tm + 40, C), F32),
                        pltpu.VMEM((tm + 32, C), F32)],
        compiler_params=_params("parallel", "parallel"),
        name="pool_mixer",
    )(u, u, u, jnp.asarray(hw).reshape(1, C), _bd4(pool_w).astype(BF16), pool_scale.reshape(1, C))


def _conv_kernel(u_ref, up_ref, un_ref, dw_ref, db_ref, lg_ref, lb_ref, pw_ref, o_ref, buf_ref, *, tm, rb):
    glu = lambda z: z[:, :GROUP] * jax.nn.sigmoid(z[:, GROUP:])
    i = pl.program_id(1)
    n = tm + 2 * HALO
    buf_ref[0, 0:HALO, :] = jnp.where(i == 0, 0.0, glu(up_ref[0]))
    buf_ref[0, HALO:HALO + tm, :] = glu(u_ref[0])
    buf_ref[0, HALO + tm:n, :] = jnp.where(i == pl.num_programs(1) - 1, 0.0, glu(un_ref[0]))
    buf_ref[0, n:n + 8, :] = jnp.zeros((8, GROUP), F32)
    for q in range(1, 8):
        buf_ref[q, 0:n, :] = buf_ref[0, q:q + n, :]
    half = CONV_WIDTH // 2
    for r0 in range(0, tm, rb):
        acc = jnp.zeros((rb, GROUP), F32) + db_ref[...]
        for j in range(CONV_WIDTH):
            lo = HALO + r0 + j - half
            acc = acc + dw_ref[j:j + 1, :] * buf_ref[lo % 8, lo - lo % 8:lo - lo % 8 + rb, :]
        mu = jnp.mean(acc, axis=-1, keepdims=True)
        xc = acc - mu
        var = jnp.mean(xc * xc, axis=-1, keepdims=True)
        hn = xc * lax.rsqrt(var + LN_EPS) * lg_ref[...] + lb_ref[...]
        o_ref[0, r0:r0 + rb, :] = _dot(hn * jax.nn.sigmoid(hn), pw_ref[...]).astype(BF16)


def _conv(u, dw_w, dw_b, ln_g, ln_b, pw, tm):
    B, T, C2 = u.shape
    C = C2 // 2
    assert C == GROUP and T % tm == 0 and HALO >= CONV_WIDTH // 2
    full = lambda shp: pl.BlockSpec(shp, lambda b, i: (0,) * len(shp))
    return pl.pallas_call(
        functools.partial(_conv_kernel, tm=tm, rb=min(tm, 128)),
        grid=(B, T // tm),
        in_specs=_halo_specs(T, tm, C2) + [full((CONV_WIDTH, C)), full((1, C)), full((1, C)), full((1, C)),
                                           full((C, C))],
        out_specs=pl.BlockSpec((1, tm, C), lambda b, i: (b, i, 0)),
        out_shape=jax.ShapeDtypeStruct((B, T, C), BF16),
        scratch_shapes=[pltpu.VMEM((8, tm + 2 * HALO + 8, C), F32)],
        compiler_params=_params("parallel", "parallel"),
        name="conv_module",
    )(u, u, u, dw_w, dw_b.reshape(1, C), ln_g.reshape(1, C), ln_b.reshape(1, C), pw.astype(BF16))


def _np_split(m):
    m = jnp.asarray(np.asarray(m, np.float32))
    hi = m.astype(BF16)
    lo = (m - hi.astype(F32)).astype(BF16)
    return hi, lo


def _dot3(a, b_hi, b_lo):
    a_hi, a_lo = _split2(a)
    return (jnp.dot(a_hi, b_hi, preferred_element_type=F32) + jnp.dot(a_lo, b_hi, preferred_element_type=F32)
            + jnp.dot(a_hi, b_lo, preferred_element_type=F32))


def _dot3c(c_hi, c_lo, a):
    a_hi, a_lo = _split2(a)
    return (jnp.dot(c_hi, a_hi, preferred_element_type=F32) + jnp.dot(c_hi, a_lo, preferred_element_type=F32)
            + jnp.dot(c_lo, a_hi, preferred_element_type=F32))


def _cos_sin(n):
    k = np.arange(n, dtype=np.float64)
    ang = 2.0 * np.pi * np.outer(k, k) / n
    return np.cos(ang), np.sin(ang)


def _chan_dft_np():
    c, s = _cos_sin(HEAD)
    return np.concatenate([_bd4_np([c] * 4), _bd4_np([s] * 4)], axis=1)


def _fourier_s1_kernel(x_ref, twc_ref, tws_ref, csh_ref, csl_ref, m1h_ref, m1l_ref, ore_ref, oim_ref,
                       *scr, tb):
    w = _dot3(x_ref[0].reshape(HEAD * tb, GROUP), csh_ref[...], csl_ref[...])
    for n, ref in enumerate(scr):
        ref[...] = w[:, n * 128:(n + 1) * 128]
    for t in range(tb):
        zz = jnp.concatenate(
            [jnp.concatenate([scr[2 * p][pl.ds(t, HEAD, stride=tb), :], scr[2 * p + 1][pl.ds(t, HEAD, stride=tb), :]],
                             axis=1) for p in range(2)], axis=0)
        y = _dot3c(m1h_ref[...], m1l_ref[...], zz)
        yre, yim = y[:HEAD], y[HEAD:]
        ct = jnp.concatenate([twc_ref[t], twc_ref[t]], axis=1)
        st = jnp.concatenate([tws_ref[t], tws_ref[t]], axis=1)
        ore_ref[0, t] = yre * ct + yim * st
        oim_ref[0, t] = yim * ct - yre * st


def _fourier_s2_kernel(yre_ref, yim_ref, c2h_ref, c2l_ref, fw_ref, o_ref, *scr, fb):
    n2 = yre_ref.shape[1]
    ins, outs = scr[:4], scr[4:]
    for p, ref in enumerate((yre_ref, yim_ref)):
        v = ref[0].reshape(n2 * fb, GROUP)
        ins[2 * p][...] = v[:, :128]
        ins[2 * p + 1][...] = v[:, 128:]
    for f in range(fb):
        rows = pl.ds(f, n2, stride=fb)
        yy = jnp.concatenate([jnp.concatenate([ins[2 * p][rows, :], ins[2 * p + 1][rows, :]], axis=1)
                              for p in range(2)], axis=0)
        res = _dot(_dot3c(c2h_ref[...], c2l_ref[...], yy), fw_ref[...])
        outs[0][rows, :] = res[:, :128]
        outs[1][rows, :] = res[:, 128:]
    o_ref[0] = jnp.concatenate([outs[0][...], outs[1][...]], axis=1).reshape(n2, fb, GROUP)


def _fourier_small_kernel(x_ref, csh_ref, csl_ref, cth_ref, ctl_ref, fw_ref, o_ref):
    w = _dot3(x_ref[0], csh_ref[...], csl_ref[...])
    zz = jnp.concatenate([w[:, :GROUP], w[:, GROUP:]], axis=0)
    o_ref[0] = _dot(_dot3c(cth_ref[...], ctl_ref[...], zz), fw_ref[...])


def _fourier(u, fourier_w, tb=8, fb=8):
    B, T, C = u.shape
    assert C == GROUP
    scale = 1.0 / math.sqrt(T * HEAD)
    csh, csl = _np_split(_chan_dft_np())
    fw = fourier_w.astype(BF16)
    full = lambda shp, n: pl.BlockSpec(shp, lambda *_: (0,) * n)
    if T <= 512:
        ct, st = _cos_sin(T)
        cth, ctl = _np_split(np.concatenate([ct, -st], axis=1) * scale)
        return pl.pallas_call(
            _fourier_small_kernel,
            grid=(B,),
            in_specs=[pl.BlockSpec((1, T, C), lambda b: (b, 0, 0)),
                      full((C, 2 * C), 2), full((C, 2 * C), 2), full((T, 2 * T), 2), full((T, 2 * T), 2),
                      full((C, C), 2)],
            out_specs=pl.BlockSpec((1, T, C), lambda b: (b, 0, 0)),
            out_shape=jax.ShapeDtypeStruct((B, T, C), F32),
            compiler_params=_params("parallel"),
            name="fourier_small",
        )(u, csh, csl, cth, ctl, fw)

    n1 = HEAD
    n2 = T // n1
    assert n1 * n2 == T and n2 % tb == 0 and n1 % fb == 0
    c1, s1 = _cos_sin(n1)
    m1h, m1l = _np_split(np.block([[c1, -s1], [-s1, -c1]]))
    f1 = np.arange(n1, dtype=np.float64)[None, :, None]
    t2 = np.arange(n2, dtype=np.float64)[:, None, None]
    ang = 2.0 * np.pi * f1 * t2 / T * np.ones((1, 1, 128))
    twc = jnp.asarray(np.cos(ang).astype(np.float32))
    tws = jnp.asarray(np.sin(ang).astype(np.float32))
    c2, s2 = _cos_sin(n2)
    c2h, c2l = _np_split(np.concatenate([c2, s2], axis=1) * scale)

    yre, yim = pl.pallas_call(
        functools.partial(_fourier_s1_kernel, tb=tb),
        grid=(n2 // tb, B),
        in_specs=[pl.BlockSpec((1, n1, tb, C), lambda j, b: (b, 0, j, 0)),
                  pl.BlockSpec((tb, n1, 128), lambda j, b: (j, 0, 0)),
                  pl.BlockSpec((tb, n1, 128), lambda j, b: (j, 0, 0)),
                  full((C, 2 * C), 2), full((C, 2 * C), 2), full((2 * n1, 2 * n1), 2), full((2 * n1, 2 * n1), 2)],
        out_specs=[pl.BlockSpec((1, tb, n1, C), lambda j, b: (b, j, 0, 0))] * 2,
        out_shape=[jax.ShapeDtypeStruct((B, n2, n1, C), F32)] * 2,
        scratch_shapes=[pltpu.VMEM((n1 * tb, 128), F32)] * 4,
        compiler_params=_params("parallel", "parallel"),
        name="fourier_stage1",
    )(u.reshape(B, n1, n2, C), twc, tws, csh, csl, m1h, m1l)

    out = pl.pallas_call(
        functools.partial(_fourier_s2_kernel, fb=fb),
        grid=(B, n1 // fb),
        in_specs=[pl.BlockSpec((1, n2, fb, C), lambda b, j: (b, 0, j, 0)),
                  pl.BlockSpec((1, n2, fb, C), lambda b, j: (b, 0, j, 0)),
                  full((n2, 2 * n2), 2), full((n2, 2 * n2), 2), full((C, C), 2)],
        out_specs=pl.BlockSpec((1, n2, fb, C), lambda b, j: (b, 0, j, 0)),
        out_shape=jax.ShapeDtypeStruct((B, n2, n1, C), F32),
        scratch_shapes=[pltpu.VMEM((n2 * fb, 128), F32)] * 6,
        compiler_params=_params("parallel", "parallel"),
        name="fourier_stage2",
    )(yre, yim, c2h, c2l, fw)
    return out.reshape(B, T, C)


def _tail_kernel(*refs, F, fc, final):
    (h_ref, y0_ref, y1_ref, y2_ref, y3_ref, wo_ref, gt1_ref, g_ref, sh_ref, sc_ref, gt2_ref,
     wi_ref, wf_ref) = refs[:13]
    o_ref = refs[-1]
    mix = None
    for n, y_ref in enumerate((y0_ref, y1_ref, y2_ref, y3_ref)):
        part = jnp.dot(y_ref[0].astype(BF16), wo_ref[n * GROUP:(n + 1) * GROUP, :], preferred_element_type=F32)
        mix = part if mix is None else mix + part
    h = h_ref[0] + gt1_ref[0] * mix
    a = _modulated(h, g_ref[...], sh_ref[0], sc_ref[0]).astype(BF16)
    acc = jnp.zeros(h.shape, F32)
    for j in range(F // fc):
        gate = jnp.dot(a, wi_ref[:, j * fc:(j + 1) * fc], preferred_element_type=F32)
        up = jnp.dot(a, wi_ref[:, F + j * fc:F + (j + 1) * fc], preferred_element_type=F32)
        mid = (gate * jax.nn.sigmoid(gate) * up).astype(BF16)
        acc = acc + jnp.dot(mid, wf_ref[j * fc:(j + 1) * fc, :], preferred_element_type=F32)
    out = h + gt2_ref[0] * acc
    if final:
        out = out * lax.rsqrt(jnp.mean(out * out, axis=-1, keepdims=True) + RMS_EPS) * refs[13][...]
    o_ref[0] = out


def _tail(h, ys, w_o, gt1, g, shift, scale, gt2, w_in, w_out, tm, final_g=None, fc=256):
    B, T, D = h.shape
    F = w_out.shape[0]
    assert F % fc == 0 and T % tm == 0
    tokd = pl.BlockSpec((1, tm, D), lambda b, i: (b, i, 0))
    tokg = pl.BlockSpec((1, tm, GROUP), lambda b, i: (b, i, 0))
    vec = pl.BlockSpec((1, 1, D), lambda b, i: (b, 0, 0))
    row = pl.BlockSpec((1, D), lambda b, i: (0, 0))
    const = lambda w: pl.BlockSpec(w.shape, lambda b, i: (0, 0), pipeline_mode=pl.Buffered(1))
    in_specs = [tokd, tokg, tokg, tokg, tokg, const(w_o), vec, row, vec, vec, vec, const(w_in), const(w_out)]
    args = [h, *ys, w_o, gt1, g.reshape(1, D), shift, scale, gt2, w_in, w_out]
    if final_g is not None:
        in_specs.append(row)
        args.append(final_g.reshape(1, D))
    return pl.pallas_call(
        functools.partial(_tail_kernel, F=F, fc=fc, final=final_g is not None),
        grid=(B, T // tm),
        in_specs=in_specs, out_specs=tokd,
        out_shape=jax.ShapeDtypeStruct((B, T, D), F32),
        compiler_params=_params("parallel", "parallel"),
        name="layer_tail",
    )(*args)


def _pos_tables(rows, dim):
    quarter = dim // 4
    omega = 1.0 / (POS_BASE ** (jnp.arange(quarter, dtype=F32) / quarter))

    def enc(p):
        ang = p[:, None] * omega[None, :]
        return jnp.concatenate([jnp.sin(ang), jnp.cos(ang)], axis=-1)

    return enc(jnp.arange(rows, dtype=F32)), enc(jnp.arange(GRID_W, dtype=F32))


def _mixers(u_pool, u_four, u_conv, lw, tm):
    pool_w, pool_scale, fourier_w, dw_w, dw_b, ln_g, ln_b, pw = lw
    return (_pool(u_pool, pool_w, pool_scale, tm), _fourier(u_four, fourier_w),
            _conv(u_conv, dw_w, dw_b, ln_g, ln_b, pw, tm))


def kernel(x, c, ctx, c_ctx, w_mod, b_mod, norm1_g, norm2_g, w_in, w_out, rwkv_mu_prev, rwkv_mu_next, rwkv_w0, rwkv_w2, rwkv_a0, rwkv_a2, rwkv_g2, rwkv_k_k, rwkv_k_a, rwkv_r_k, rwkv_gn_w, rwkv_gn_b, pool_w, pool_scale, fourier_w, conv_dw_w, conv_dw_b, conv_ln_g, conv_ln_b, conv_pw, ffn_w_in, ffn_w_out, final_norm_g):
    B, T, D = x.shape
    Tc = ctx.shape[1]
    depth = w_mod.shape[0]
    widths = (RWKV_COLS, GROUP, GROUP, 2 * GROUP)
    tm, tmc = 512, Tc

    cc = jnp.concatenate([c, c_ctx[None], jnp.zeros((8 - B - 1, D), F32)], axis=0)
    mod = _mod(cc, w_mod, b_mod)
    pos = _pos_tables(T // GRID_W, D)
    s_zero = jnp.zeros((B, 2, HEAD, GROUP), F32)

    h, hc = x, ctx
    for l in range(depth):
        last = l == depth - 1
        mx = [mod[l, :B, n * D:(n + 1) * D][:, None, :] for n in range(6)]
        mc = [jnp.broadcast_to(mod[l, B, n * D:(n + 1) * D][None, None, :], (B, 1, D)) for n in range(6)]
        w_in_b = w_in[l].astype(BF16)
        w_out_b = w_out[l].astype(BF16)
        ffn_in_b = ffn_w_in[l].astype(BF16)
        ffn_out_b = ffn_w_out[l].astype(BF16)
        rw = (rwkv_mu_prev[l], rwkv_mu_next[l], rwkv_w0[l], rwkv_w2[l], rwkv_a0[l], rwkv_a2[l], rwkv_g2[l],
              rwkv_k_k[l], rwkv_k_a[l], rwkv_r_k[l].reshape(-1), rwkv_gn_w[l], rwkv_gn_b[l])
        ow = (pool_w[l], pool_scale[l], fourier_w[l], conv_dw_w[l], conv_dw_b[l], conv_ln_g[l], conv_ln_b[l],
              conv_pw[l])

        if l == 0:
            h, ux_r, ux_p, ux_f, ux_c = _proj(h, norm1_g[l], mx[0], mx[1], w_in_b, widths, tm, pos=pos)
        else:
            ux_r, ux_p, ux_f, ux_c = _proj(h, norm1_g[l], mx[0], mx[1], w_in_b, widths, tm)
        if last:
            (uc_r,) = _proj(hc, norm1_g[l], mc[0], mc[1], w_in_b[:, :RWKV_COLS], (RWKV_COLS,), tmc)
        else:
            uc_r, uc_p, uc_f, uc_c = _proj(hc, norm1_g[l], mc[0], mc[1], w_in_b, widths, tmc)

        yc_r, s_ctx = _rwkv_stream(uc_r, s_zero, rw, tmc, tmc, Tc // CHUNK)
        yx_r, _ = _rwkv_stream(ux_r, s_ctx, rw, tm, 2 * tm, 16)

        yx = (yx_r,) + _mixers(ux_p, ux_f, ux_c, ow, tm)
        h = _tail(h, yx, w_out_b, mx[2], norm2_g[l], mx[3], mx[4], mx[5], ffn_in_b, ffn_out_b, tm,
                  final_g=final_norm_g if last else None)
        if not last:
            yc = (yc_r,) + _mixers(uc_p, uc_f, uc_c, ow, tmc)
            hc = _tail(hc, yc, w_out_b, mc[2], norm2_g[l], mc[3], mc[4], mc[5], ffn_in_b, ffn_out_b, tmc)
    return h
```
